```python
import math
import jax, jax.numpy as jnp
from jax import lax
import numpy as np

D_MODEL = 4096
BATCH = 2
SEQ = 4096
DEPTH = 4

CHUNK = 64
Q_BLOCK = 2 * CHUNK
N_HEADS = 32
HEAD_DIM = D_MODEL // N_HEADS
D_FF = 2 * D_MODEL
N_EXPERTS = 8
TOP_K = 2
D_FF_EXPERT = D_MODEL // 2
N_MOD = 6
N_A_LAYERS = DEPTH // 2
N_B_LAYERS = DEPTH - N_A_LAYERS
N_DENSE = (DEPTH + 1) // 2
N_MOE = DEPTH // 2
DEEPNORM_ALPHA = (2.0 * DEPTH) ** 0.25
DEEPNORM_BETA = (8.0 * DEPTH) ** -0.25
LN_EPS = 1e-5
ADA_SCALE = 0.1
FORGET_BIAS = 3.0
NEG_INF = -1e30

kernel_name = "hybrid_stickbreak_fox_yoco_deepnorm_adaln_moe"


def _layernorm(x, g, b):
    xf = x.astype(jnp.float32)
    mu = jnp.mean(xf, axis=-1, keepdims=True)
    var = jnp.mean(jnp.square(xf - mu), axis=-1, keepdims=True)
    y = (xf - mu) * lax.rsqrt(var + LN_EPS)
    return (y * g.astype(jnp.float32) + b.astype(jnp.float32)).astype(x.dtype)


def _modulate(x, shift, scale):
    return x * (1.0 + scale[:, None, :]) + shift[:, None, :]


def _split_heads(t):
    b, s, _ = t.shape
    return t.reshape(b, s, N_HEADS, HEAD_DIM).transpose(0, 2, 1, 3).astype(jnp.float32)


def _merge_heads(t, dtype):
    b, h, s, d = t.shape
    return t.transpose(0, 2, 1, 3).reshape(b, s, h * d).astype(dtype)


def _query_blocks(t):
    b, h, s = t.shape[:3]
    t = t.reshape((b, h, s // Q_BLOCK, Q_BLOCK) + t.shape[3:])
    return jnp.moveaxis(t, 2, 0)


def _unblock(o):
    nb, b, h, qb, d = o.shape
    return jnp.moveaxis(o, 0, 2).reshape(b, h, nb * qb, d)


def _stick_breaking_attention(q, k, v):
    s_len = k.shape[2]
    scale = HEAD_DIM ** -0.5
    key_pos = jnp.arange(s_len)
    nb = s_len // Q_BLOCK

    def block(args):
        qb, start = args
        z = jnp.einsum('bhqd,bhkd->bhqk', qb, k) * scale
        q_pos = start + jnp.arange(Q_BLOCK)
        mask = key_pos[None, :] < q_pos[:, None]
        log_keep = jnp.where(mask, jax.nn.log_sigmoid(-z), 0.0)
        suffix = lax.cumsum(log_keep, axis=3, reverse=True) - log_keep
        w = jnp.where(mask, jnp.exp(jax.nn.log_sigmoid(z) + suffix), 0.0)
        return jnp.einsum('bhqk,bhkd->bhqd', w, v)

    starts = jnp.arange(nb, dtype=jnp.int32) * Q_BLOCK
    out = lax.map(block, (_query_blocks(q), starts))
    return _unblock(out)


def _forgetting_attention(q, k, v, log_f_cum):
    s_len = k.shape[2]
    scale = HEAD_DIM ** -0.5
    key_pos = jnp.arange(s_len)
    nb = s_len // Q_BLOCK

    def block(args):
        qb, fq, start = args
        logits = jnp.einsum('bhqd,bhkd->bhqk', qb, k) * scale
        logits = logits + fq[..., None] - log_f_cum[:, :, None, :]
        q_pos = start + jnp.arange(Q_BLOCK)
        mask = key_pos[None, :] <= q_pos[:, None]
        p = jax.nn.softmax(jnp.where(mask, logits, NEG_INF), axis=-1)
        return jnp.einsum('bhqk,bhkd->bhqd', p, v)

    starts = jnp.arange(nb, dtype=jnp.int32) * Q_BLOCK
    out = lax.map(block, (_query_blocks(q), _query_blocks(log_f_cum), starts))
    return _unblock(out)


def _shared_kv(x, mod, kv_table, kv_w, kv_b_f):
    h = _modulate(x, mod[:, 0] + kv_table[0], mod[:, 1] + kv_table[1])
    kvf = h @ kv_w
    k = _split_heads(kvf[..., :D_MODEL])
    v = _split_heads(kvf[..., D_MODEL:2 * D_MODEL])
    f_logit = (kvf[..., 2 * D_MODEL:] + kv_b_f).astype(jnp.float32)
    log_f_cum = lax.cumsum(jax.nn.log_sigmoid(f_logit), axis=1)
    return k, v, log_f_cum.transpose(0, 2, 1)


def _swiglu(h, w_up, w_down):
    g, u = jnp.split(h @ w_up, 2, axis=-1)
    return (jax.nn.silu(g) * u) @ w_down


def _moe_swiglu(h, w_router, b_router, w_up, w_down):
    b, s, d = h.shape
    t = h.reshape(b * s, d)
    logits = (t @ w_router).astype(jnp.float32) + b_router.astype(jnp.float32)
    top_vals, top_idx = lax.top_k(logits, TOP_K)
    top_w = jax.nn.softmax(top_vals, axis=-1)
    gates = jnp.sum(jax.nn.one_hot(top_idx, N_EXPERTS, dtype=jnp.float32) * top_w[..., None], axis=1)
    gates = gates.astype(h.dtype)
    g, u = jnp.split(jnp.einsum('td,edf->tef', t, w_up), 2, axis=-1)
    act = jax.nn.silu(g) * u * gates[:, :, None]
    y = jnp.einsum('tef,efd->td', act, w_down)
    return y.reshape(b, s, d)


def setup_inputs(seed: int = 0) -> dict:
    key = jax.random.key(seed)
    ks = jax.random.split(key, 23)
    s = D_MODEL ** -0.5

    def nrm(k, shape, std):
        return jax.random.normal(k, shape, jnp.float32) * std

    x = nrm(ks[0], (BATCH, SEQ, D_MODEL), 1.0)
    c = nrm(ks[1], (BATCH, D_MODEL), 1.0)
    ada_w = nrm(ks[2], (D_MODEL, N_MOD * D_MODEL), ADA_SCALE * s)
    ada_b = nrm(ks[3], (N_MOD * D_MODEL,), 0.01)
    ada_table = nrm(ks[4], (DEPTH, N_MOD, D_MODEL), 0.02)
    kv_table = nrm(ks[5], (2, D_MODEL), 0.02)
    a_w_qkv = jnp.concatenate([
        nrm(ks[6], (N_A_LAYERS, D_MODEL, 2 * D_MODEL), s),
        nrm(ks[7], (N_A_LAYERS, D_MODEL, D_MODEL), s * DEEPNORM_BETA)], axis=-1)
    a_w_o = nrm(ks[8], (N_A_LAYERS, D_MODEL, D_MODEL), s * DEEPNORM_BETA)
    kv_w = jnp.concatenate([
        nrm(ks[9], (D_MODEL, D_MODEL), s),
        nrm(ks[10], (D_MODEL, D_MODEL), s * DEEPNORM_BETA),
        nrm(ks[11], (D_MODEL, N_HEADS), s)], axis=-1)
    kv_b_f = FORGET_BIAS + nrm(ks[12], (N_HEADS,), 0.1)
    b_w_q = nrm(ks[13], (N_B_LAYERS, D_MODEL, D_MODEL), s)
    b_w_o = nrm(ks[14], (N_B_LAYERS, D_MODEL, D_MODEL), s * DEEPNORM_BETA)
    ln_g = 1.0 + nrm(ks[15], (DEPTH, 2, D_MODEL), 0.02)
    ln_b = nrm(ks[16], (DEPTH, 2, D_MODEL), 0.02)
    ffn_w_up = nrm(ks[17], (N_DENSE, D_MODEL, 2 * D_FF), s)
    ffn_w_down = nrm(ks[18], (N_DENSE, D_FF, D_MODEL), D_FF ** -0.5 * DEEPNORM_BETA)
    moe_w_router = nrm(ks[19], (N_MOE, D_MODEL, N_EXPERTS), s)
    moe_b_router = nrm(ks[20], (N_MOE, N_EXPERTS), 0.01)
    moe_w_up = nrm(ks[21], (N_MOE, N_EXPERTS, D_MODEL, 2 * D_FF_EXPERT), s)
    moe_w_down = nrm(ks[22], (N_MOE, N_EXPERTS, D_FF_EXPERT, D_MODEL), D_FF_EXPERT ** -0.5 * DEEPNORM_BETA)
    return {"x": x, "c": c, "ada_w": ada_w, "ada_b": ada_b, "ada_table": ada_table,
            "kv_table": kv_table, "a_w_qkv": a_w_qkv, "a_w_o": a_w_o, "kv_w": kv_w,
            "kv_b_f": kv_b_f, "b_w_q": b_w_q, "b_w_o": b_w_o, "ln_g": ln_g, "ln_b": ln_b,
            "ffn_w_up": ffn_w_up, "ffn_w_down": ffn_w_down, "moe_w_router": moe_w_router,
            "moe_b_router": moe_b_router, "moe_w_up": moe_w_up, "moe_w_down": moe_w_down}


def reference(x, c, ada_w, ada_b, ada_table, kv_table, a_w_qkv, a_w_o, kv_w, kv_b_f,
              b_w_q, b_w_o, ln_g, ln_b, ffn_w_up, ffn_w_down, moe_w_router, moe_b_router,
              moe_w_up, moe_w_down):
    bsz = x.shape[0]
    mod = (jax.nn.silu(c) @ ada_w + ada_b).reshape(bsz, N_MOD, D_MODEL)
    shared = None
    for l in range(DEPTH):
        m = mod + ada_table[l][None]
        shift_mix, scale_mix, gate_mix, shift_ffn, scale_ffn, gate_ffn = (m[:, i] for i in range(N_MOD))

        h = _modulate(x, shift_mix, scale_mix)
        if l < N_A_LAYERS:
            q, k, v = jnp.split(h @ a_w_qkv[l], 3, axis=-1)
            o = _stick_breaking_attention(_split_heads(q), _split_heads(k), _split_heads(v))
            mix = _merge_heads(o, x.dtype) @ a_w_o[l]
        else:
            if l == N_A_LAYERS:
                shared = _shared_kv(x, mod, kv_table, kv_w, kv_b_f)
            k_sh, v_sh, log_f_cum = shared
            j = l - N_A_LAYERS
            q = _split_heads(h @ b_w_q[j])
            o = _forgetting_attention(q, k_sh, v_sh, log_f_cum)
            mix = _merge_heads(o, x.dtype) @ b_w_o[j]
        x = _layernorm(DEEPNORM_ALPHA * x + (1.0 + gate_mix[:, None, :]) * mix, ln_g[l, 0], ln_b[l, 0])

        h = _modulate(x, shift_ffn, scale_ffn)
        if l % 2 == 0:
            y = _swiglu(h, ffn_w_up[l // 2], ffn_w_down[l // 2])
        else:
            y = _moe_swiglu(h, moe_w_router[l // 2], moe_b_router[l // 2],
                            moe_w_up[l // 2], moe_w_down[l // 2])
        x = _layernorm(DEEPNORM_ALPHA * x + (1.0 + gate_ffn[:, None, :]) * y, ln_g[l, 1], ln_b[l, 1])
    return x
```

```python
import functools

import jax
import jax.numpy as jnp
from jax import lax
from jax.experimental import pallas as pl
from jax.experimental.pallas import tpu as pltpu

F32 = jnp.float32
BF16 = jnp.bfloat16

LN_EPS = 1e-5
TOP_K = 2
NEG_INF = -1e30
FORGET_LANES = 128
ROUTER_LANES = 128

V7X_VMEM_BYTES = 64 * 1024 * 1024
VMEM_HEADROOM_BYTES = 12 * 1024 * 1024
VMEM_CAP_BYTES = V7X_VMEM_BYTES - 6 * 1024 * 1024


def _params(semantics, block_bytes):
    limit = min(int(block_bytes) + VMEM_HEADROOM_BYTES, VMEM_CAP_BYTES)
    return pltpu.CompilerParams(dimension_semantics=semantics, vmem_limit_bytes=limit)


def _nbytes(shape, dtype):
    n = 1
    for s in shape:
        n *= s
    return n * jnp.dtype(dtype).itemsize


def _pick(n, prefs):
    for p in prefs:
        if n % p == 0:
            return p
    return n


def _ada_kernel(c_ref, w_ref, b_ref, o_ref):
    c = c_ref[...]
    s = c * jax.nn.sigmoid(c)
    o_ref[...] = jnp.dot(s.astype(BF16), w_ref[...].astype(BF16),
                         preferred_element_type=F32) + b_ref[...]


def _ada_mod(c, ada_w, ada_b):
    bsz, d = c.shape
    n = ada_w.shape[1]
    rows = 8
    c8 = jnp.zeros((rows, d), F32).at[:bsz].set(c)
    tn = _pick(n, (512, 256, 128))
    blk = 2 * (_nbytes((rows, d), F32) + _nbytes((d, tn), F32) + 2 * _nbytes((rows, tn), F32))
    out = pl.pallas_call(
        _ada_kernel,
        grid=(n // tn,),
        in_specs=[pl.BlockSpec((rows, d), lambda j: (0, 0)),
                  pl.BlockSpec((d, tn), lambda j: (0, j)),
                  pl.BlockSpec((1, tn), lambda j: (0, j))],
        out_specs=pl.BlockSpec((rows, tn), lambda j: (0, j)),
        out_shape=jax.ShapeDtypeStruct((rows, n), F32),
        compiler_params=_params(("arbitrary",), blk),
        name="ada_mod",
    )(c8, ada_w, ada_b.reshape(1, n))
    return out[:bsz]


def _modcast_kernel(x_ref, sh_ref, sc_ref, o_ref):
    o_ref[...] = (x_ref[...] * sc_ref[0] + sh_ref[0]).astype(o_ref.dtype)


def _modcast(x2, shift, scale1, seq, out_dtype):
    t, d = x2.shape
    tm = _pick(seq, (256, 128, 64, 32, 16, 8))
    tpb = seq // tm
    vec = pl.BlockSpec((1, 1, d), lambda i: (i // tpb, 0, 0))
    blk = 2 * (_nbytes((tm, d), F32) + _nbytes((tm, d), out_dtype))
    return pl.pallas_call(
        _modcast_kernel,
        grid=(t // tm,),
        in_specs=[pl.BlockSpec((tm, d), lambda i: (i, 0)), vec, vec],
        out_specs=pl.BlockSpec((tm, d), lambda i: (i, 0)),
        out_shape=jax.ShapeDtypeStruct((t, d), out_dtype),
        compiler_params=_params(("arbitrary",), blk),
        name="modcast",
    )(x2, shift, scale1)


def _mm_kernel(a_ref, w_ref, o_ref, wb_ref):
    @pl.when(pl.program_id(1) == 0)
    def _():
        wb_ref[...] = w_ref[...].astype(BF16)

    o_ref[...] = jnp.dot(a_ref[...], wb_ref[...],
                         preferred_element_type=F32).astype(o_ref.dtype)


def _mm_tiles(m, k, n):
    tn = _pick(n, (512, 256, 128)) if k <= 4096 else _pick(n, (256, 128))
    tm = _pick(m, (1024, 512, 256, 128, 64, 32, 16, 8)) if k <= 4096 else \
        _pick(m, (512, 256, 128, 64, 32, 16, 8))
    return tm, tn


def _matmul(a, w, w_index, col0, n, out_dtype):
    m, k = a.shape
    tm, tn = _mm_tiles(m, k, n)
    cb0 = col0 // tn
    if w_index is None:
        w_spec = pl.BlockSpec((k, tn), lambda j, i: (0, cb0 + j))
    else:
        w_spec = pl.BlockSpec((None, k, tn), lambda j, i: (w_index, 0, cb0 + j))
    blk = (2 * (_nbytes((tm, k), BF16) + _nbytes((k, tn), F32) + _nbytes((tm, tn), out_dtype))
           + _nbytes((k, tn), BF16))
    return pl.pallas_call(
        _mm_kernel,
        grid=(n // tn, m // tm),
        in_specs=[pl.BlockSpec((tm, k), lambda j, i: (i, 0)), w_spec],
        out_specs=pl.BlockSpec((tm, tn), lambda j, i: (i, j)),
        out_shape=jax.ShapeDtypeStruct((m, n), out_dtype),
        scratch_shapes=[pltpu.VMEM((k, tn), BF16)],
        compiler_params=_params(("arbitrary", "arbitrary"), blk),
        name="matmul",
    )(a, w)


def _swiglu_kernel(a_ref, wg_ref, wu_ref, o_ref, wgb_ref, wub_ref):
    @pl.when(pl.program_id(1) == 0)
    def _():
        wgb_ref[...] = wg_ref[...].astype(BF16)
        wub_ref[...] = wu_ref[...].astype(BF16)

    a = a_ref[...]
    g = jnp.dot(a, wgb_ref[...], preferred_element_type=F32)
    u = jnp.dot(a, wub_ref[...], preferred_element_type=F32)
    o_ref[...] = (g * jax.nn.sigmoid(g) * u).astype(o_ref.dtype)


def _swiglu_up(a, w_up, w_index):
    m, k = a.shape
    f = w_up.shape[-1] // 2
    tn = _pick(f, (256, 128))
    tm = _pick(m, (1024, 512, 256, 128, 64, 32, 16, 8))
    nb = f // tn
    blk = (2 * (_nbytes((tm, k), BF16) + 2 * _nbytes((k, tn), F32) + _nbytes((tm, tn), BF16))
           + 2 * _nbytes((k, tn), BF16))
    return pl.pallas_call(
        _swiglu_kernel,
        grid=(nb, m // tm),
        in_specs=[pl.BlockSpec((tm, k), lambda j, i: (i, 0)),
                  pl.BlockSpec((None, k, tn), lambda j, i: (w_index, 0, j)),
                  pl.BlockSpec((None, k, tn), lambda j, i: (w_index, 0, nb + j))],
        out_specs=pl.BlockSpec((tm, tn), lambda j, i: (i, j)),
        out_shape=jax.ShapeDtypeStruct((m, f), BF16),
        scratch_shapes=[pltpu.VMEM((k, tn), BF16), pltpu.VMEM((k, tn), BF16)],
        compiler_params=_params(("arbitrary", "arbitrary"), blk),
        name="swiglu_up",
    )(a, w_up, w_up)


def _expert_changed(te_ref, i):
    prev = te_ref[jnp.maximum(i - 1, 0)]
    return jnp.logical_or(i == 0, te_ref[i] != prev)


def _gswiglu_kernel(te_ref, tv_ref, a_ref, wg_ref, wu_ref, gate_ref, o_ref, wgb_ref, wub_ref):
    i = pl.program_id(1)

    @pl.when(_expert_changed(te_ref, i))
    def _():
        wgb_ref[...] = wg_ref[...].astype(BF16)
        wub_ref[...] = wu_ref[...].astype(BF16)

    @pl.when(tv_ref[i] > 0)
    def _():
        a = a_ref[...]
        g = jnp.dot(a, wgb_ref[...], preferred_element_type=F32)
        u = jnp.dot(a, wub_ref[...], preferred_element_type=F32)
        o_ref[...] = (g * jax.nn.sigmoid(g) * u * gate_ref[...]).astype(o_ref.dtype)

    @pl.when(tv_ref[i] == 0)
    def _():
        o_ref[...] = jnp.zeros_like(o_ref)


def _grouped_swiglu_up(a, w_up, w_index, row_gate, tile_expert, tile_valid, tile):
    p, k = a.shape
    f = w_up.shape[-1] // 2
    tn = _pick(f, (256, 128))
    nb = f // tn
    blk = (2 * (_nbytes((tile, k), BF16) + 2 * _nbytes((k, tn), F32) + _nbytes((tile, tn), BF16)
                + _nbytes((tile, 128), F32))
           + 2 * _nbytes((k, tn), BF16))
    grid_spec = pltpu.PrefetchScalarGridSpec(
        num_scalar_prefetch=2,
        grid=(nb, p // tile),
        in_specs=[pl.BlockSpec((tile, k), lambda j, i, te, tv: (i, 0)),
                  pl.BlockSpec((None, None, k, tn), lambda j, i, te, tv: (w_index, te[i], 0, j)),
                  pl.BlockSpec((None, None, k, tn),
                               lambda j, i, te, tv: (w_index, te[i], 0, nb + j)),
                  pl.BlockSpec((tile, 1), lambda j, i, te, tv: (i, 0))],
        out_specs=pl.BlockSpec((tile, tn), lambda j, i, te, tv: (i, j)),
        scratch_shapes=[pltpu.VMEM((k, tn), BF16), pltpu.VMEM((k, tn), BF16)],
    )
    return pl.pallas_call(
        _gswiglu_kernel,
        grid_spec=grid_spec,
        out_shape=jax.ShapeDtypeStruct((p, f), BF16),
        compiler_params=_params(("arbitrary", "arbitrary"), blk),
        name="expert_swiglu_up",
    )(tile_expert, tile_valid, a, w_up, w_up, row_gate)


def _gmm_kernel(te_ref, tv_ref, a_ref, w_ref, o_ref, wb_ref):
    i = pl.program_id(1)

    @pl.when(_expert_changed(te_ref, i))
    def _():
        wb_ref[...] = w_ref[...].astype(BF16)

    @pl.when(tv_ref[i] > 0)
    def _():
        o_ref[...] = jnp.dot(a_ref[...], wb_ref[...],
                             preferred_element_type=F32).astype(o_ref.dtype)

    @pl.when(tv_ref[i] == 0)
    def _():
        o_ref[...] = jnp.zeros_like(o_ref)


def _grouped_matmul(a, w, w_index, tile_expert, tile_valid, tile, out_dtype):
    p, k = a.shape
    n = w.shape[-1]
    tn = _pick(n, (512, 256, 128))
    blk = (2 * (_nbytes((tile, k), BF16) + _nbytes((k, tn), F32) + _nbytes((tile, tn), out_dtype))
           + _nbytes((k, tn), BF16))
    grid_spec = pltpu.PrefetchScalarGridSpec(
        num_scalar_prefetch=2,
        grid=(n // tn, p // tile),
        in_specs=[pl.BlockSpec((tile, k), lambda j, i, te, tv: (i, 0)),
                  pl.BlockSpec((None, None, k, tn), lambda j, i, te, tv: (w_index, te[i], 0, j))],
        out_specs=pl.BlockSpec((tile, tn), lambda j, i, te, tv: (i, j)),
        scratch_shapes=[pltpu.VMEM((k, tn), BF16)],
    )
    return pl.pallas_call(
        _gmm_kernel,
        grid_spec=grid_spec,
        out_shape=jax.ShapeDtypeStruct((p, n), out_dtype),
        compiler_params=_params(("arbitrary", "arbitrary"), blk),
        name="expert_matmul",
    )(tile_expert, tile_valid, a, w)


def _softplus(z):
    return jnp.maximum(z, 0.0) + jnp.log(1.0 + jnp.exp(-jnp.abs(z)))


def _stick_kernel(q_ref, k_ref, v_ref, o_ref, acc_ref, carry_ref, *, tq, tk, scale):
    qi = pl.program_id(2)
    n_sub = tq // tk
    q = q_ref[0]
    row = lax.broadcasted_iota(jnp.int32, (tk, tk), 0)
    col = lax.broadcasted_iota(jnp.int32, (tk, tk), 1)
    later = (row > col).astype(BF16)
    acc_ref[...] = jnp.zeros_like(acc_ref)
    carry_ref[...] = jnp.zeros_like(carry_ref)

    def block(kb, masked):
        start = pl.multiple_of(kb * tk, tk)
        k = k_ref[0, pl.ds(start, tk), :]
        v = v_ref[0, pl.ds(start, tk), :]
        z = lax.dot_general(q, k, (((1,), (1,)), ((), ())),
                            preferred_element_type=F32) * scale
        sp = _softplus(z)
        if masked:
            q_pos = qi * tq + lax.broadcasted_iota(jnp.int32, (tq, tk), 0)
            k_pos = start + lax.broadcasted_iota(jnp.int32, (tq, tk), 1)
            mask = k_pos < q_pos
            log_keep = jnp.where(mask, -sp, 0.0)
        else:
            log_keep = -sp
        hi = log_keep.astype(BF16)
        lo = (log_keep - hi.astype(F32)).astype(BF16)
        suffix = (jnp.dot(hi, later, preferred_element_type=F32)
                  + jnp.dot(lo, later, preferred_element_type=F32))
        carry = carry_ref[...]
        w = jnp.exp(z - sp + suffix + carry)
        if masked:
            w = jnp.where(mask, w, 0.0)
        acc_ref[...] += jnp.dot(w.astype(BF16), v, preferred_element_type=F32)
        carry_ref[...] = carry + jnp.sum(log_keep, axis=1, keepdims=True)

    for s in range(n_sub - 1, -1, -1):
        block(qi * n_sub + s, True)

    def body(j, c):
        block(qi * n_sub - 1 - j, False)
        return c

    lax.fori_loop(0, qi * n_sub, body, 0)
    o_ref[0] = acc_ref[...].astype(o_ref.dtype)


def _stick_attention(qkv, n_heads, head_dim, out_dtype):
    bsz, seq, _ = qkv.shape
    d = n_heads * head_dim
    tq = _pick(seq, (256, 128))
    tk = tq
    kern = functools.partial(_stick_kernel, tq=tq, tk=tk, scale=head_dim ** -0.5)
    blk = (2 * (2 * _nbytes((tq, head_dim), BF16) + 2 * _nbytes((seq, head_dim), BF16))
           + 2 * _nbytes((tq, 128), F32) + 10 * _nbytes((tq, tk), F32))
    return pl.pallas_call(
        kern,
        grid=(bsz, n_heads, seq // tq),
        in_specs=[pl.BlockSpec((1, tq, head_dim), lambda b, h, i: (b, i, h)),
                  pl.BlockSpec((1, seq, head_dim), lambda b, h, i: (b, 0, n_heads + h)),
                  pl.BlockSpec((1, seq, head_dim), lambda b, h, i: (b, 0, 2 * n_heads + h))],
        out_specs=pl.BlockSpec((1, tq, head_dim), lambda b, h, i: (b, i, h)),
        out_shape=jax.ShapeDtypeStruct((bsz, seq, d), out_dtype),
        scratch_shapes=[pltpu.VMEM((tq, head_dim), F32), pltpu.VMEM((tq, 1), F32)],
        compiler_params=_params(("arbitrary", "arbitrary", "arbitrary"), blk),
        name="stick_attention",
    )(qkv, qkv, qkv)


def _fox_kernel(q_ref, k_ref, v_ref, fq_ref, fk_ref, o_ref, m_ref, l_ref, acc_ref,
                *, tq, tk, scale):
    qi = pl.program_id(2)
    n_sub = tq // tk
    q = q_ref[0]
    fq = fq_ref[0, 0]
    m_ref[...] = jnp.full_like(m_ref, NEG_INF)
    l_ref[...] = jnp.zeros_like(l_ref)
    acc_ref[...] = jnp.zeros_like(acc_ref)

    def block(kb, masked):
        start = pl.multiple_of(kb * tk, tk)
        k = k_ref[0, pl.ds(start, tk), :]
        v = v_ref[0, pl.ds(start, tk), :]
        fk = fk_ref[0, 0, pl.ds(kb, 1), :]
        s = lax.dot_general(q, k, (((1,), (1,)), ((), ())), preferred_element_type=F32)
        logits = s * scale + fq - fk
        if masked:
            q_pos = qi * tq + lax.broadcasted_iota(jnp.int32, (tq, tk), 0)
            k_pos = start + lax.broadcasted_iota(jnp.int32, (tq, tk), 1)
            logits = jnp.where(k_pos <= q_pos, logits, NEG_INF)
        m_prev = m_ref[...]
        m_new = jnp.maximum(m_prev, jnp.max(logits, axis=1, keepdims=True))
        alpha = jnp.exp(m_prev - m_new)
        p = jnp.exp(logits - m_new)
        l_ref[...] = alpha * l_ref[...] + jnp.sum(p, axis=1, keepdims=True)
        acc_ref[...] = alpha * acc_ref[...] + jnp.dot(p.astype(BF16), v,
                                                      preferred_element_type=F32)
        m_ref[...] = m_new

    def body(j, c):
        block(j, False)
        return c

    lax.fori_loop(0, qi * n_sub, body, 0)
    for s in range(n_sub):
        block(qi * n_sub + s, True)
    o_ref[0] = (acc_ref[...] / l_ref[...]).astype(o_ref.dtype)


def _fox_attention(q, kv, log_f_cum, n_heads, head_dim, out_dtype):
    bsz, seq, d = q.shape
    tq = _pick(seq, (256, 128))
    tk = tq
    fq = log_f_cum.reshape(bsz, n_heads, seq, 1)
    fk = log_f_cum.reshape(bsz, n_heads, seq // tk, tk)
    kern = functools.partial(_fox_kernel, tq=tq, tk=tk, scale=head_dim ** -0.5)
    blk = (2 * (2 * _nbytes((tq, head_dim), BF16) + 2 * _nbytes((seq, head_dim), BF16)
                + _nbytes((tq, 128), F32) + _nbytes((seq // tk, tk), F32))
           + 3 * _nbytes((tq, 128), F32) + 8 * _nbytes((tq, tk), F32))
    return pl.pallas_call(
        kern,
        grid=(bsz, n_heads, seq // tq),
        in_specs=[pl.BlockSpec((1, tq, head_dim), lambda b, h, i: (b, i, h)),
                  pl.BlockSpec((1, seq, head_dim), lambda b, h, i: (b, 0, h)),
                  pl.BlockSpec((1, seq, head_dim), lambda b, h, i: (b, 0, n_heads + h)),
                  pl.BlockSpec((1, 1, tq, 1), lambda b, h, i: (b, h, i, 0)),
                  pl.BlockSpec((1, 1, seq // tk, tk), lambda b, h, i: (b, h, 0, 0))],
        out_specs=pl.BlockSpec((1, tq, head_dim), lambda b, h, i: (b, i, h)),
        out_shape=jax.ShapeDtypeStruct((bsz, seq, d), out_dtype),
        scratch_shapes=[pltpu.VMEM((tq, 1), F32), pltpu.VMEM((tq, 1), F32),
                        pltpu.VMEM((tq, head_dim), F32)],
        compiler_params=_params(("arbitrary", "arbitrary", "arbitrary"), blk),
        name="fox_attention",
    )(q, kv, kv, fq, fk)


def _forget_kernel(h_ref, w_ref, b_ref, o_ref, carry_ref, *, ts):
    @pl.when(pl.program_id(1) == 0)
    def _():
        carry_ref[...] = jnp.zeros_like(carry_ref)

    logit = jnp.dot(h_ref[0], w_ref[...].astype(BF16), preferred_element_type=F32) + b_ref[...]
    log_f = -_softplus(-logit)
    row = lax.broadcasted_iota(jnp.int32, (ts, ts), 0)
    col = lax.broadcasted_iota(jnp.int32, (ts, ts), 1)
    upto = (col <= row).astype(F32)
    csum = jnp.dot(upto, log_f, preferred_element_type=F32,
                   precision=lax.Precision.HIGHEST) + carry_ref[...]
    o_ref[0] = csum
    carry_ref[...] = csum[ts - 1:ts, :]


def _forget_cumsum(h3, w_f, b_f):
    bsz, seq, d = h3.shape
    n_heads = w_f.shape[1]
    lanes = FORGET_LANES
    w_pad = jnp.zeros((d, lanes), F32).at[:, :n_heads].set(w_f)
    b_pad = jnp.zeros((1, lanes), F32).at[0, :n_heads].set(b_f.astype(F32))
    ts = _pick(seq, (256, 128, 64, 32, 16, 8))
    blk = 2 * (_nbytes((ts, d), BF16) + _nbytes((d, lanes), F32) + _nbytes((ts, lanes), F32)) \
        + 4 * _nbytes((ts, ts), F32)
    out = pl.pallas_call(
        functools.partial(_forget_kernel, ts=ts),
        grid=(bsz, seq // ts),
        in_specs=[pl.BlockSpec((1, ts, d), lambda b, i: (b, i, 0)),
                  pl.BlockSpec((d, lanes), lambda b, i: (0, 0)),
                  pl.BlockSpec((1, lanes), lambda b, i: (0, 0))],
        out_specs=pl.BlockSpec((1, ts, lanes), lambda b, i: (b, i, 0)),
        out_shape=jax.ShapeDtypeStruct((bsz, seq, lanes), F32),
        scratch_shapes=[pltpu.VMEM((1, lanes), F32)],
        compiler_params=_params(("arbitrary", "arbitrary"), blk),
        name="forget_cumsum",
    )(h3, w_pad, b_pad)
    return out[:, :, :n_heads]


def _ln_rows(v, g, b):
    mu = jnp.mean(v, axis=-1, keepdims=True)
    cen = v - mu
    var = jnp.mean(cen * cen, axis=-1, keepdims=True)
    return cen * lax.rsqrt(var + LN_EPS) * g + b


def _emit_mods(xn, mod_refs, out_refs):
    for m in range(len(out_refs)):
        sh_ref, sc_ref = mod_refs[2 * m], mod_refs[2 * m + 1]
        out_refs[m][...] = (xn * sc_ref[0] + sh_ref[0]).astype(out_refs[m].dtype)


def _ln_kernel(*refs, alpha, n_mod):
    x_ref, y_ref, gate_ref, g_ref, b_ref = refs[:5]
    mod_refs = refs[5:5 + 2 * n_mod]
    xo_ref = refs[5 + 2 * n_mod]
    h_refs = refs[6 + 2 * n_mod:]
    v = alpha * x_ref[...] + gate_ref[0] * y_ref[...].astype(F32)
    xn = _ln_rows(v, g_ref[...], b_ref[...])
    xo_ref[...] = xn
    _emit_mods(xn, mod_refs, h_refs)


def _residual_ln(x2, y2, gate1, ln_g, ln_b, mods, mod_dtypes, seq, alpha):
    t, d = x2.shape
    n_mod = len(mods)
    tm = _pick(seq, (256, 128, 64, 32, 16, 8))
    tpb = seq // tm
    tile = pl.BlockSpec((tm, d), lambda i: (i, 0))
    vec = pl.BlockSpec((1, 1, d), lambda i: (i // tpb, 0, 0))
    par = pl.BlockSpec((1, d), lambda i: (0, 0))
    in_specs = [tile, tile, vec, par, par] + [vec, vec] * n_mod
    args = [x2, y2, gate1, ln_g.reshape(1, d), ln_b.reshape(1, d)]
    for sh, sc1 in mods:
        args += [sh, sc1]
    out_shape = [jax.ShapeDtypeStruct((t, d), F32)] + \
        [jax.ShapeDtypeStruct((t, d), dt) for dt in mod_dtypes]
    blk = 2 * (_nbytes((tm, d), F32) + _nbytes((tm, d), y2.dtype) + _nbytes((tm, d), F32)
               + sum(_nbytes((tm, d), dt) for dt in mod_dtypes)) + 4 * _nbytes((tm, d), F32)
    outs = pl.pallas_call(
        functools.partial(_ln_kernel, alpha=alpha, n_mod=n_mod),
        grid=(t // tm,),
        in_specs=in_specs,
        out_specs=[tile] * (1 + n_mod),
        out_shape=out_shape,
        compiler_params=_params(("arbitrary",), blk),
        name="residual_ln",
    )(*args)
    return outs[0], list(outs[1:])


def _router_kernel(h_ref, w_ref, b_ref, idx_ref, wgt_ref, *, n_experts):
    logits = jnp.dot(h_ref[...], w_ref[...], preferred_element_type=F32,
                     precision=lax.Precision.HIGHEST) + b_ref[...]
    lane = lax.broadcasted_iota(jnp.int32, logits.shape, 1)
    logits = jnp.where(lane < n_experts, logits, -jnp.inf)
    big = jnp.int32(ROUTER_LANES)
    v1 = jnp.max(logits, axis=1, keepdims=True)
    i1 = jnp.min(jnp.where(logits == v1, lane, big), axis=1, keepdims=True)
    rest = jnp.where(lane == i1, -jnp.inf, logits)
    v2 = jnp.max(rest, axis=1, keepdims=True)
    i2 = jnp.min(jnp.where(rest == v2, lane, big), axis=1, keepdims=True)
    e2 = jnp.exp(v2 - v1)
    w1 = 1.0 / (1.0 + e2)
    w2 = e2 / (1.0 + e2)
    idx_ref[...] = jnp.where(lane == 0, i1, jnp.where(lane == 1, i2, 0))
    wgt_ref[...] = jnp.where(lane == 0, w1, jnp.where(lane == 1, w2, 0.0))


def _router(h32, w_router, b_router):
    t, d = h32.shape
    n_experts = w_router.shape[1]
    lanes = ROUTER_LANES
    w_pad = jnp.zeros((d, lanes), F32).at[:, :n_experts].set(w_router)
    b_pad = jnp.zeros((1, lanes), F32).at[0, :n_experts].set(b_router.astype(F32))
    tm = _pick(t, (256, 128, 64, 32, 16, 8))
    blk = 2 * (_nbytes((tm, d), F32) + _nbytes((d, lanes), F32) + 2 * _nbytes((tm, lanes), F32)) \
        + 4 * _nbytes((tm, d), F32)
    idx, wgt = pl.pallas_call(
        functools.partial(_router_kernel, n_experts=n_experts),
        grid=(t // tm,),
        in_specs=[pl.BlockSpec((tm, d), lambda i: (i, 0)),
                  pl.BlockSpec((d, lanes), lambda i: (0, 0)),
                  pl.BlockSpec((1, lanes), lambda i: (0, 0))],
        out_specs=[pl.BlockSpec((tm, lanes), lambda i: (i, 0)),
                   pl.BlockSpec((tm, lanes), lambda i: (i, 0))],
        out_shape=[jax.ShapeDtypeStruct((t, lanes), jnp.int32),
                   jax.ShapeDtypeStruct((t, lanes), F32)],
        compiler_params=_params(("arbitrary",), blk),
        name="router",
    )(h32, w_pad, b_pad)
    return idx[:, :TOP_K], wgt[:, :TOP_K]


def _route_plan(top_idx, top_w, n_experts, tile):
    t = top_idx.shape[0]
    p = TOP_K * t
    e_flat = top_idx.reshape(p)
    onehot = (e_flat[:, None] == jnp.arange(n_experts, dtype=jnp.int32)[None, :]).astype(jnp.int32)
    csum = jnp.cumsum(onehot, axis=0)
    counts = csum[-1]
    rank = jnp.sum(csum * onehot, axis=1) - 1
    padded = ((counts + tile - 1) // tile) * tile
    gend = jnp.cumsum(padded)
    gstart = gend - padded
    pos = jnp.sum(onehot * gstart[None, :], axis=1) + rank
    n_rows = ((p + tile - 1) // tile + n_experts) * tile
    row_token = jnp.zeros((n_rows,), jnp.int32).at[pos].set(
        jnp.arange(p, dtype=jnp.int32) // TOP_K)
    row_gate = jnp.zeros((n_rows,), F32).at[pos].set(top_w.reshape(p))
    tile_start = jnp.arange(n_rows // tile, dtype=jnp.int32) * tile
    tile_expert = jnp.sum((tile_start[:, None] >= gend[None, :]).astype(jnp.int32), axis=1)
    tile_valid = (tile_start < gend[-1]).astype(jnp.int32)
    tile_expert = jnp.minimum(tile_expert, n_experts - 1)
    return pos.astype(jnp.int32), row_token, row_gate.reshape(n_rows, 1), tile_expert, tile_valid


def _row_copy(src_ref, src_row, dst_ref, dst_row, sem):
    return pltpu.make_async_copy(src_ref.at[pl.ds(src_row, 1)], dst_ref.at[pl.ds(dst_row, 1)], sem)


def _dispatch_kernel(tok_ref, src_ref, o_ref, buf_ref, sem, *, rows):
    base = pl.program_id(0) * rows

    def issue(r, c):
        _row_copy(src_ref, tok_ref[base + r], buf_ref, r, sem).start()
        return c

    lax.fori_loop(0, rows, issue, 0)

    def drain(r, c):
        _row_copy(src_ref, 0, buf_ref, r, sem).wait()
        return c

    lax.fori_loop(0, rows, drain, 0)
    o_ref[...] = buf_ref[...].astype(o_ref.dtype)


def _dispatch(h32, row_token, rows):
    t, d = h32.shape
    n_rows = row_token.shape[0]
    blk = 2 * _nbytes((rows, d), BF16) + 2 * _nbytes((rows, d), F32)
    grid_spec = pltpu.PrefetchScalarGridSpec(
        num_scalar_prefetch=1,
        grid=(n_rows // rows,),
        in_specs=[pl.BlockSpec(memory_space=pl.ANY)],
        out_specs=pl.BlockSpec((rows, d), lambda i, tok: (i, 0)),
        scratch_shapes=[pltpu.VMEM((rows, d), F32), pltpu.SemaphoreType.DMA(())],
    )
    return pl.pallas_call(
        functools.partial(_dispatch_kernel, rows=rows),
        grid_spec=grid_spec,
        out_shape=jax.ShapeDtypeStruct((n_rows, d), BF16),
        compiler_params=_params(("arbitrary",), blk),
        name="expert_dispatch",
    )(row_token, h32)


def _combine_ln_kernel(*refs, alpha, n_mod, tm):
    pos_ref, x_ref, rows_ref, gate_ref, g_ref, b_ref = refs[:6]
    mod_refs = refs[6:6 + 2 * n_mod]
    xo_ref = refs[6 + 2 * n_mod]
    h_refs = refs[7 + 2 * n_mod:7 + 3 * n_mod]
    buf_ref, sem = refs[7 + 3 * n_mod:]
    base = pl.program_id(0) * tm

    def issue(r, c):
        for k in range(TOP_K):
            _row_copy(rows_ref, pos_ref[(base + r) * TOP_K + k], buf_ref.at[k], r, sem).start()
        return c

    lax.fori_loop(0, tm, issue, 0)

    def drain(r, c):
        for k in range(TOP_K):
            _row_copy(rows_ref, 0, buf_ref.at[k], r, sem).wait()
        return c

    lax.fori_loop(0, tm, drain, 0)
    y = buf_ref[0]
    for k in range(1, TOP_K):
        y = y + buf_ref[k]
    v = alpha * x_ref[...] + gate_ref[0] * y
    xn = _ln_rows(v, g_ref[...], b_ref[...])
    xo_ref[...] = xn
    _emit_mods(xn, mod_refs, h_refs)


def _combine_ln(x2, expert_rows, pos, gate1, ln_g, ln_b, mods, mod_dtypes, seq, alpha):
    t, d = x2.shape
    n_mod = len(mods)
    tm = _pick(seq, (256, 128, 64, 32, 16, 8))
    tpb = seq // tm
    tile = pl.BlockSpec((tm, d), lambda i, pos: (i, 0))
    vec = pl.BlockSpec((1, 1, d), lambda i, pos: (i // tpb, 0, 0))
    par = pl.BlockSpec((1, d), lambda i, pos: (0, 0))
    in_specs = [tile, pl.BlockSpec(memory_space=pl.ANY), vec, par, par] + [vec, vec] * n_mod
    args = [x2, expert_rows, gate1, ln_g.reshape(1, d), ln_b.reshape(1, d)]
    for sh, sc1 in mods:
        args += [sh, sc1]
    out_shape = [jax.ShapeDtypeStruct((t, d), F32)] + \
        [jax.ShapeDtypeStruct((t, d), dt) for dt in mod_dtypes]
    blk = 2 * (2 * _nbytes((tm, d), F32) + sum(_nbytes((tm, d), dt) for dt in mod_dtypes)) \
        + (TOP_K + 4) * _nbytes((tm, d), F32)
    grid_spec = pltpu.PrefetchScalarGridSpec(
        num_scalar_prefetch=1,
        grid=(t // tm,),
        in_specs=in_specs,
        out_specs=[tile] * (1 + n_mod),
        scratch_shapes=[pltpu.VMEM((TOP_K, tm, d), F32), pltpu.SemaphoreType.DMA(())],
    )
    outs = pl.pallas_call(
        functools.partial(_combine_ln_kernel, alpha=alpha, n_mod=n_mod, tm=tm),
        grid_spec=grid_spec,
        out_shape=out_shape,
        compiler_params=_params(("arbitrary",), blk),
        name="combine_ln",
    )(pos, *args)
    return outs[0], list(outs[1:])


def kernel(x, c, ada_w, ada_b, ada_table, kv_table, a_w_qkv, a_w_o, kv_w, kv_b_f, b_w_q, b_w_o,
           ln_g, ln_b, ffn_w_up, ffn_w_down, moe_w_router, moe_b_router, moe_w_up, moe_w_down):
    bsz, seq, d = x.shape
    t = bsz * seq
    depth, n_mod = ada_table.shape[0], ada_table.shape[1]
    n_a = a_w_qkv.shape[0]
    n_heads = kv_w.shape[1] - 2 * d
    head_dim = d // n_heads
    n_experts = moe_w_router.shape[-1]
    alpha = (2.0 * depth) ** 0.25
    expert_tile = _pick(TOP_K * t, (512, 256, 128, 64, 32, 16))

    mod = _ada_mod(c, ada_w, ada_b).reshape(bsz, n_mod, d)

    def vec3(v):
        return v.reshape(bsz, 1, d)

    def layer_mods(l):
        m = mod + ada_table[l][None]
        return [m[:, i] for i in range(n_mod)]

    def mix_mod(l):
        m = layer_mods(l)
        return vec3(m[0]), vec3(1.0 + m[1])

    kv_mod = (vec3(mod[:, 0] + kv_table[0]), vec3(1.0 + mod[:, 1] + kv_table[1]))

    x2 = x.reshape(t, d)
    h = _modcast(x2, *mix_mod(0), seq, BF16)
    h_kv = None
    shared = None
    for l in range(depth):
        m = layer_mods(l)
        gate_mix1, gate_ffn1 = vec3(1.0 + m[2]), vec3(1.0 + m[5])
        ffn_mod = (vec3(m[3]), vec3(1.0 + m[4]))
        is_moe = l % 2 == 1

        if l < n_a:
            qkv = _matmul(h, a_w_qkv, l, 0, 3 * d, BF16)
            o = _stick_attention(qkv.reshape(bsz, seq, 3 * d), n_heads, head_dim, BF16)
            mix = _matmul(o.reshape(t, d), a_w_o, l, 0, d, F32)
        else:
            if shared is None:
                kv = _matmul(h_kv, kv_w, None, 0, 2 * d, BF16)
                log_f = _forget_cumsum(h_kv.reshape(bsz, seq, d), kv_w[:, 2 * d:], kv_b_f)
                shared = (kv.reshape(bsz, seq, 2 * d), jnp.transpose(log_f, (0, 2, 1)))
            j = l - n_a
            q = _matmul(h, b_w_q, j, 0, d, BF16)
            o = _fox_attention(q.reshape(bsz, seq, d), shared[0], shared[1],
                               n_heads, head_dim, BF16)
            mix = _matmul(o.reshape(t, d), b_w_o, j, 0, d, F32)
        x2, (h_ffn,) = _residual_ln(x2, mix, gate_mix1, ln_g[l, 0], ln_b[l, 0], [ffn_mod],
                                    [F32 if is_moe else BF16], seq, alpha)

        next_mods, next_dtypes = [], []
        if l + 1 < depth:
            next_mods.append(mix_mod(l + 1))
            next_dtypes.append(BF16)
            if l + 1 == n_a:
                next_mods.append(kv_mod)
                next_dtypes.append(BF16)
        if not is_moe:
            act = _swiglu_up(h_ffn, ffn_w_up, l // 2)
            y = _matmul(act, ffn_w_down, l // 2, 0, d, F32)
            x2, hs = _residual_ln(x2, y, gate_ffn1, ln_g[l, 1], ln_b[l, 1], next_mods,
                                  next_dtypes, seq, alpha)
        else:
            e = l // 2
            top_idx, top_w = _router(h_ffn, moe_w_router[e], moe_b_router[e])
            pos, row_token, row_gate, tile_expert, tile_valid = _route_plan(
                top_idx, top_w, n_experts, expert_tile)
            hg = _dispatch(h_ffn, row_token, min(expert_tile, 256))
            act = _grouped_swiglu_up(hg, moe_w_up, e, row_gate, tile_expert, tile_valid,
                                     expert_tile)
            rows = _grouped_matmul(act, moe_w_down, e, tile_expert, tile_valid, expert_tile, F32)
            x2, hs = _combine_ln(x2, rows, pos, gate_ffn1, ln_g[l, 1], ln_b[l, 1], next_mods,
                                 next_dtypes, seq, alpha)
        if hs:
            h = hs[0]
            if len(hs) > 1:
                h_kv = hs[1]
    return x2.reshape(bsz, seq, d)
```

```python
import functools

import jax
import jax.numpy as jnp
from jax import lax
from jax.experimental import pallas as pl
from jax.experimental.pallas import tpu as pltpu

F32 = jnp.float32
BF16 = jnp.bfloat16

LN_EPS = 1e-5
TOP_K = 2
NEG_INF = -1e30
LOG2E = 1.4426950408889634
LANE = 128
FORGET_LANES = LANE
ROUTER_LANES = LANE
ATTN_HEADS_PER_STEP = 4

V7X_VMEM_BYTES = 64 * 1024 * 1024
VMEM_HEADROOM_BYTES = 12 * 1024 * 1024
VMEM_CAP_BYTES = V7X_VMEM_BYTES - 6 * 1024 * 1024


def _params(semantics, block_bytes):
    limit = min(int(block_bytes) + VMEM_HEADROOM_BYTES, VMEM_CAP_BYTES)
    return pltpu.CompilerParams(dimension_semantics=semantics, vmem_limit_bytes=limit)


def _nbytes(shape, dtype):
    n = 1
    for s in shape:
        n *= s
    return n * jnp.dtype(dtype).itemsize


def _pick(n, prefs):
    for p in prefs:
        if n % p == 0:
            return p
    return n


def _ada_kernel(c_ref, w_ref, b_ref, o_ref):
    c = c_ref[...]
    s = c * jax.nn.sigmoid(c)
    o_ref[...] = jnp.dot(s.astype(BF16), w_ref[...].astype(BF16),
                         preferred_element_type=F32) + b_ref[...]


def _ada_mod(c, ada_w, ada_b):
    bsz, d = c.shape
    n = ada_w.shape[1]
    rows = 8
    c8 = jnp.zeros((rows, d), F32).at[:bsz].set(c)
    tn = _pick(n, (512, 256, 128))
    blk = 2 * (_nbytes((rows, d), F32) + _nbytes((d, tn), F32) + 2 * _nbytes((rows, tn), F32))
    out = pl.pallas_call(
        _ada_kernel,
        grid=(n // tn,),
        in_specs=[pl.BlockSpec((rows, d), lambda j: (0, 0)),
                  pl.BlockSpec((d, tn), lambda j: (0, j)),
                  pl.BlockSpec((1, tn), lambda j: (0, j))],
        out_specs=pl.BlockSpec((rows, tn), lambda j: (0, j)),
        out_shape=jax.ShapeDtypeStruct((rows, n), F32),
        compiler_params=_params(("arbitrary",), blk),
        name="ada_mod",
    )(c8, ada_w, ada_b.reshape(1, n))
    return out[:bsz]


def _modcast_kernel(x_ref, sh_ref, sc_ref, o_ref):
    o_ref[...] = (x_ref[...] * sc_ref[0] + sh_ref[0]).astype(o_ref.dtype)


def _modcast(x2, shift, scale1, seq, out_dtype):
    t, d = x2.shape
    tm = _pick(seq, (256, 128, 64, 32, 16, 8))
    tpb = seq // tm
    vec = pl.BlockSpec((1, 1, d), lambda i: (i // tpb, 0, 0))
    blk = 2 * (_nbytes((tm, d), F32) + _nbytes((tm, d), out_dtype))
    return pl.pallas_call(
        _modcast_kernel,
        grid=(t // tm,),
        in_specs=[pl.BlockSpec((tm, d), lambda i: (i, 0)), vec, vec],
        out_specs=pl.BlockSpec((tm, d), lambda i: (i, 0)),
        out_shape=jax.ShapeDtypeStruct((t, d), out_dtype),
        compiler_params=_params(("arbitrary",), blk),
        name="modcast",
    )(x2, shift, scale1)


def _mm_kernel(a_ref, w_ref, o_ref, wb_ref, *, scaled_blocks, col_scale):
    @pl.when(pl.program_id(1) == 0)
    def _():
        wb_ref[...] = w_ref[...].astype(BF16)

    res = jnp.dot(a_ref[...], wb_ref[...], preferred_element_type=F32)
    if scaled_blocks:
        res = res * jnp.where(pl.program_id(0) < scaled_blocks, col_scale, 1.0)
    o_ref[...] = res.astype(o_ref.dtype)


MM_BLOCK_ELEMS = 4 * 1024 * 1024


def _mm_tiles(m, k, n):
    tn = _pick(n, (512, 256, 128))
    tm = _pick(m, tuple(t for t in (1024, 512, 256, 128, 64, 32, 16, 8) if t * k <= MM_BLOCK_ELEMS))
    w_buffers = 2 if k * tn <= MM_BLOCK_ELEMS // 2 else 1
    return tm, tn, w_buffers


def _matmul(a, w, w_index, col0, n, out_dtype, scaled_cols=0, col_scale=1.0):
    m, k = a.shape
    tm, tn, w_buffers = _mm_tiles(m, k, n)
    cb0 = col0 // tn
    assert scaled_cols % tn == 0
    mode = pl.Buffered(w_buffers)
    if w_index is None:
        w_spec = pl.BlockSpec((k, tn), lambda j, i: (0, cb0 + j), pipeline_mode=mode)
    else:
        w_spec = pl.BlockSpec((None, k, tn), lambda j, i: (w_index, 0, cb0 + j),
                              pipeline_mode=mode)
    blk = (2 * (_nbytes((tm, k), BF16) + _nbytes((tm, tn), out_dtype))
           + w_buffers * _nbytes((k, tn), F32) + _nbytes((k, tn), BF16))
    return pl.pallas_call(
        functools.partial(_mm_kernel, scaled_blocks=scaled_cols // tn, col_scale=col_scale),
        grid=(n // tn, m // tm),
        in_specs=[pl.BlockSpec((tm, k), lambda j, i: (i, 0)), w_spec],
        out_specs=pl.BlockSpec((tm, tn), lambda j, i: (i, j)),
        out_shape=jax.ShapeDtypeStruct((m, n), out_dtype),
        scratch_shapes=[pltpu.VMEM((k, tn), BF16)],
        compiler_params=_params(("arbitrary", "arbitrary"), blk),
        name="matmul",
    )(a, w)


def _swiglu_kernel(a_ref, wg_ref, wu_ref, o_ref, wgb_ref, wub_ref):
    @pl.when(pl.program_id(1) == 0)
    def _():
        wgb_ref[...] = wg_ref[...].astype(BF16)
        wub_ref[...] = wu_ref[...].astype(BF16)

    a = a_ref[...]
    g = jnp.dot(a, wgb_ref[...], preferred_element_type=F32)
    u = jnp.dot(a, wub_ref[...], preferred_element_type=F32)
    o_ref[...] = (g * jax.nn.sigmoid(g) * u).astype(o_ref.dtype)


def _swiglu_up(a, w_up, w_index):
    m, k = a.shape
    f = w_up.shape[-1] // 2
    tn = _pick(f, (256, 128))
    tm = _pick(m, (1024, 512, 256, 128, 64, 32, 16, 8))
    nb = f // tn
    blk = (2 * (_nbytes((tm, k), BF16) + 2 * _nbytes((k, tn), F32) + _nbytes((tm, tn), BF16))
           + 2 * _nbytes((k, tn), BF16))
    return pl.pallas_call(
        _swiglu_kernel,
        grid=(nb, m // tm),
        in_specs=[pl.BlockSpec((tm, k), lambda j, i: (i, 0)),
                  pl.BlockSpec((None, k, tn), lambda j, i: (w_index, 0, j)),
                  pl.BlockSpec((None, k, tn), lambda j, i: (w_index, 0, nb + j))],
        out_specs=pl.BlockSpec((tm, tn), lambda j, i: (i, j)),
        out_shape=jax.ShapeDtypeStruct((m, f), BF16),
        scratch_shapes=[pltpu.VMEM((k, tn), BF16), pltpu.VMEM((k, tn), BF16)],
        compiler_params=_params(("arbitrary", "arbitrary"), blk),
        name="swiglu_up",
    )(a, w_up, w_up)


def _expert_changed(te_ref, i):
    prev = te_ref[jnp.maximum(i - 1, 0)]
    return jnp.logical_or(i == 0, te_ref[i] != prev)


def _gswiglu_kernel(te_ref, tv_ref, a_ref, wg_ref, wu_ref, o_ref, wgb_ref, wub_ref):
    i = pl.program_id(1)

    @pl.when(_expert_changed(te_ref, i))
    def _():
        wgb_ref[...] = wg_ref[...].astype(BF16)
        wub_ref[...] = wu_ref[...].astype(BF16)

    @pl.when(tv_ref[i] > 0)
    def _():
        a = a_ref[...]
        g = jnp.dot(a, wgb_ref[...], preferred_element_type=F32)
        u = jnp.dot(a, wub_ref[...], preferred_element_type=F32)
        o_ref[...] = (g * jax.nn.sigmoid(g) * u).astype(o_ref.dtype)

    @pl.when(tv_ref[i] == 0)
    def _():
        o_ref[...] = jnp.zeros_like(o_ref)


def _grouped_swiglu_up(a, w_up, w_index, tile_expert, tile_valid, tile):
    p, k = a.shape
    f = w_up.shape[-1] // 2
    tn = _pick(f, (256, 128))
    nb = f // tn
    blk = (2 * (_nbytes((tile, k), BF16) + 2 * _nbytes((k, tn), F32) + _nbytes((tile, tn), BF16))
           + 2 * _nbytes((k, tn), BF16))
    grid_spec = pltpu.PrefetchScalarGridSpec(
        num_scalar_prefetch=2,
        grid=(nb, p // tile),
        in_specs=[pl.BlockSpec((tile, k), lambda j, i, te, tv: (i, 0)),
                  pl.BlockSpec((None, None, k, tn), lambda j, i, te, tv: (w_index, te[i], 0, j)),
                  pl.BlockSpec((None, None, k, tn),
                               lambda j, i, te, tv: (w_index, te[i], 0, nb + j))],
        out_specs=pl.BlockSpec((tile, tn), lambda j, i, te, tv: (i, j)),
        scratch_shapes=[pltpu.VMEM((k, tn), BF16), pltpu.VMEM((k, tn), BF16)],
    )
    return pl.pallas_call(
        _gswiglu_kernel,
        grid_spec=grid_spec,
        out_shape=jax.ShapeDtypeStruct((p, f), BF16),
        compiler_params=_params(("arbitrary", "arbitrary"), blk),
        name="expert_swiglu_up",
    )(tile_expert, tile_valid, a, w_up, w_up)


def _gmm_kernel(te_ref, tv_ref, a_ref, w_ref, o_ref, wb_ref):
    i = pl.program_id(1)

    @pl.when(_expert_changed(te_ref, i))
    def _():
        wb_ref[...] = w_ref[...].astype(BF16)

    @pl.when(tv_ref[i] > 0)
    def _():
        o_ref[...] = jnp.dot(a_ref[...], wb_ref[...],
                             preferred_element_type=F32).astype(o_ref.dtype)

    @pl.when(tv_ref[i] == 0)
    def _():
        o_ref[...] = jnp.zeros_like(o_ref)


def _grouped_matmul(a, w, w_index, tile_expert, tile_valid, tile, out_dtype):
    p, k = a.shape
    n = w.shape[-1]
    tn = _pick(n, (512, 256, 128))
    blk = (2 * (_nbytes((tile, k), BF16) + _nbytes((k, tn), F32) + _nbytes((tile, tn), out_dtype))
           + _nbytes((k, tn), BF16))
    grid_spec = pltpu.PrefetchScalarGridSpec(
        num_scalar_prefetch=2,
        grid=(n // tn, p // tile),
        in_specs=[pl.BlockSpec((tile, k), lambda j, i, te, tv: (i, 0)),
                  pl.BlockSpec((None, None, k, tn), lambda j, i, te, tv: (w_index, te[i], 0, j))],
        out_specs=pl.BlockSpec((tile, tn), lambda j, i, te, tv: (i, j)),
        scratch_shapes=[pltpu.VMEM((k, tn), BF16)],
    )
    return pl.pallas_call(
        _gmm_kernel,
        grid_spec=grid_spec,
        out_shape=jax.ShapeDtypeStruct((p, n), out_dtype),
        compiler_params=_params(("arbitrary", "arbitrary"), blk),
        name="expert_matmul",
    )(tile_expert, tile_valid, a, w)


def _softplus(z):
    return jnp.maximum(z, 0.0) + jnp.log(1.0 + jnp.exp(-jnp.abs(z)))


def _neg_abs(x):
    bits = lax.bitcast_convert_type(x, jnp.uint32) | jnp.uint32(0x80000000)
    return lax.bitcast_convert_type(bits, F32)


def _lanes(x, n):
    reps = n // x.shape[1]
    return x if reps == 1 else jnp.concatenate([x] * reps, axis=1)


def _stick_kernel(q_ref, k_ref, v_ref, o_ref, acc_ref, carry_ref, *, tq, tk, heads, hd):
    qi = pl.program_id(2)
    n_sub = tq // tk
    row = lax.broadcasted_iota(jnp.int32, (tk, tk), 0)
    col = lax.broadcasted_iota(jnp.int32, (tk, tk), 1)
    later = (row > col).astype(BF16)
    acc_ref[...] = jnp.zeros_like(acc_ref)
    carry_ref[...] = jnp.zeros_like(carry_ref)

    def block(g, r0, k0, diag_off):
        nr = tq - r0
        lanes = slice(g * hd, (g + 1) * hd)
        q = q_ref[0, r0:tq, lanes]
        k = k_ref[0, pl.ds(k0, tk), lanes]
        v = v_ref[0, pl.ds(k0, tk), lanes]
        s = lax.dot_general(q, k, (((1,), (1,)), ((), ())), preferred_element_type=F32)
        soft = jnp.log(1.0 + jnp.exp2(_neg_abs(s))) * LOG2E
        log_keep = jnp.minimum(s, 0.0) - soft
        if diag_off is not None:
            q_pos = r0 + lax.broadcasted_iota(jnp.int32, (nr, tk), 0)
            k_pos = diag_off + lax.broadcasted_iota(jnp.int32, (nr, tk), 1)
            mask = k_pos < q_pos
            log_keep = jnp.where(mask, log_keep, 0.0)
        suffix = jnp.dot(log_keep.astype(BF16), later, preferred_element_type=F32)
        carry = carry_ref[g, r0:tq, :]
        w = jnp.exp2((log_keep - s) + suffix + _lanes(carry, tk))
        if diag_off is not None:
            w = jnp.where(mask, w, 0.0)
        acc_ref[g, r0:tq, :] += jnp.dot(w.astype(BF16), v, preferred_element_type=F32)
        carry_ref[g, r0:tq, :] = carry + jnp.sum(log_keep, axis=1, keepdims=True)

    diag0 = qi * tq
    for sub in range(n_sub - 1, -1, -1):
        for g in range(heads):
            block(g, sub * tk, pl.multiple_of(diag0 + sub * tk, tk), sub * tk)

    def body(j, c):
        k0 = pl.multiple_of(diag0 - (j + 1) * tk, tk)
        for g in range(heads):
            block(g, 0, k0, None)
        return c

    lax.fori_loop(0, qi * n_sub, body, 0)
    for g in range(heads):
        o_ref[0, :, g * hd:(g + 1) * hd] = acc_ref[g].astype(o_ref.dtype)


def _attn_tiles(seq):
    tq = _pick(seq, (512, 256, 128))
    tk = min(tq, 256)
    return tq, tk


def _stick_attention(qkv, n_heads, head_dim, out_dtype, heads_per_step):
    bsz, seq, _ = qkv.shape
    d = n_heads * head_dim
    g = heads_per_step
    tq, tk = _attn_tiles(seq)
    width = g * head_dim
    kern = functools.partial(_stick_kernel, tq=tq, tk=tk, heads=g, hd=head_dim)
    blk = (2 * (2 * _nbytes((tq, width), BF16) + 2 * _nbytes((seq, width), BF16))
           + g * (_nbytes((tq, head_dim), F32) + _nbytes((tq, LANE), F32))
           + 8 * g * _nbytes((tq, tk), F32))
    groups = n_heads // g
    return pl.pallas_call(
        kern,
        grid=(bsz, groups, seq // tq),
        in_specs=[pl.BlockSpec((1, tq, width), lambda b, h, i: (b, i, h)),
                  pl.BlockSpec((1, seq, width), lambda b, h, i: (b, 0, groups + h)),
                  pl.BlockSpec((1, seq, width), lambda b, h, i: (b, 0, 2 * groups + h))],
        out_specs=pl.BlockSpec((1, tq, width), lambda b, h, i: (b, i, h)),
        out_shape=jax.ShapeDtypeStruct((bsz, seq, d), out_dtype),
        scratch_shapes=[pltpu.VMEM((g, tq, head_dim), F32), pltpu.VMEM((g, tq, LANE), F32)],
        compiler_params=_params(("arbitrary", "arbitrary", "arbitrary"), blk),
        name="stick_attention",
    )(qkv, qkv, qkv)


def _fox_kernel(q_ref, k_ref, v_ref, fk_ref, o_ref, m_ref, fqb_ref, acc_ref,
                *, tq, tk, heads, hd):
    qi = pl.program_id(2)
    n_sub = tq // tk
    m_ref[...] = jnp.full_like(m_ref, NEG_INF)
    acc_ref[...] = jnp.zeros_like(acc_ref)
    for g in range(heads):
        for sub in range(n_sub):
            f_row = fk_ref[0, g, pl.ds(qi * n_sub + sub, 1), :]
            for c in range(tk // LANE):
                chunk = f_row[:, c * LANE:(c + 1) * LANE]
                r0 = sub * tk + c * LANE
                fqb_ref[g, r0:r0 + LANE, :] = jnp.broadcast_to(chunk, (LANE, LANE)).T
    ones = jnp.ones((tk, LANE), BF16)

    def block(g, r0, k0, kb, diag_off):
        nr = tq - r0
        lanes = slice(g * hd, (g + 1) * hd)
        q = q_ref[0, r0:tq, lanes]
        k = k_ref[0, pl.ds(k0, tk), lanes]
        v1 = jnp.concatenate([v_ref[0, pl.ds(k0, tk), lanes], ones], axis=1)
        fk = fk_ref[0, g, pl.ds(kb, 1), :]
        u = lax.dot_general(q, k, (((1,), (1,)), ((), ())), preferred_element_type=F32) - fk
        if diag_off is not None:
            q_pos = r0 + lax.broadcasted_iota(jnp.int32, (nr, tk), 0)
            k_pos = diag_off + lax.broadcasted_iota(jnp.int32, (nr, tk), 1)
            u = jnp.where(k_pos <= q_pos, u, NEG_INF)
        fq = fqb_ref[g, r0:tq, :]
        m_prev = m_ref[g, r0:tq, :]
        m_new = jnp.maximum(m_prev, jnp.max(u, axis=1, keepdims=True) + fq)
        alpha = jnp.exp2(m_prev - m_new)
        p = jnp.exp2(u + _lanes(fq - m_new, tk))
        pv = jnp.dot(p.astype(BF16), v1, preferred_element_type=F32)
        acc_ref[g, r0:tq, :] = _lanes(alpha, hd + LANE) * acc_ref[g, r0:tq, :] + pv
        m_ref[g, r0:tq, :] = m_new

    def body(j, c):
        k0 = pl.multiple_of(j * tk, tk)
        for g in range(heads):
            block(g, 0, k0, j, None)
        return c

    lax.fori_loop(0, qi * n_sub, body, 0)
    for sub in range(n_sub):
        for g in range(heads):
            block(g, sub * tk, pl.multiple_of(qi * tq + sub * tk, tk), qi * n_sub + sub, sub * tk)
    for g in range(heads):
        acc = acc_ref[g]
        o_ref[0, :, g * hd:(g + 1) * hd] = (acc[:, :hd] / acc[:, hd:]).astype(o_ref.dtype)


def _fox_attention(q, kv, log_f_cum, n_heads, head_dim, out_dtype, heads_per_step):
    bsz, seq, d = q.shape
    g = heads_per_step
    assert head_dim == LANE
    tq, tk = _attn_tiles(seq)
    width = g * head_dim
    groups = n_heads // g
    fk = log_f_cum.reshape(bsz, n_heads, seq // tk, tk)
    kern = functools.partial(_fox_kernel, tq=tq, tk=tk, heads=g, hd=head_dim)
    blk = (2 * (2 * _nbytes((tq, width), BF16) + 2 * _nbytes((seq, width), BF16)
                + g * _nbytes((seq // tk, tk), F32))
           + 4 * g * _nbytes((tq, LANE), F32) + 8 * g * _nbytes((tq, tk), F32))
    return pl.pallas_call(
        kern,
        grid=(bsz, groups, seq // tq),
        in_specs=[pl.BlockSpec((1, tq, width), lambda b, h, i: (b, i, h)),
                  pl.BlockSpec((1, seq, width), lambda b, h, i: (b, 0, h)),
                  pl.BlockSpec((1, seq, width), lambda b, h, i: (b, 0, groups + h)),
                  pl.BlockSpec((1, g, seq // tk, tk), lambda b, h, i: (b, h, 0, 0))],
        out_specs=pl.BlockSpec((1, tq, width), lambda b, h, i: (b, i, h)),
        out_shape=jax.ShapeDtypeStruct((bsz, seq, d), out_dtype),
        scratch_shapes=[pltpu.VMEM((g, tq, LANE), F32), pltpu.VMEM((g, tq, LANE), F32),
                        pltpu.VMEM((g, tq, head_dim + LANE), F32)],
        compiler_params=_params(("arbitrary", "arbitrary", "arbitrary"), blk),
        name="fox_attention",
    )(q, kv, kv, fk)


def _forget_kernel(h_ref, w_ref, b_ref, o_ref, carry_ref, *, ts):
    @pl.when(pl.program_id(1) == 0)
    def _():
        carry_ref[...] = jnp.zeros_like(carry_ref)

    logit = jnp.dot(h_ref[0], w_ref[...].astype(BF16), preferred_element_type=F32) + b_ref[...]
    log_f = -_softplus(-logit) * LOG2E
    row = lax.broadcasted_iota(jnp.int32, (ts, ts), 0)
    col = lax.broadcasted_iota(jnp.int32, (ts, ts), 1)
    upto = (col <= row).astype(F32)
    csum = jnp.dot(upto, log_f, preferred_element_type=F32,
                   precision=lax.Precision.HIGHEST) + carry_ref[...]
    o_ref[0] = csum
    carry_ref[...] = csum[ts - 1:ts, :]


def _forget_cumsum(h3, w_f, b_f):
    bsz, seq, d = h3.shape
    n_heads = w_f.shape[1]
    lanes = FORGET_LANES
    w_pad = jnp.zeros((d, lanes), F32).at[:, :n_heads].set(w_f)
    b_pad = jnp.zeros((1, lanes), F32).at[0, :n_heads].set(b_f.astype(F32))
    ts = _pick(seq, (256, 128, 64, 32, 16, 8))
    blk = 2 * (_nbytes((ts, d), BF16) + _nbytes((d, lanes), F32) + _nbytes((ts, lanes), F32)) \
        + 4 * _nbytes((ts, ts), F32)
    out = pl.pallas_call(
        functools.partial(_forget_kernel, ts=ts),
        grid=(bsz, seq // ts),
        in_specs=[pl.BlockSpec((1, ts, d), lambda b, i: (b, i, 0)),
                  pl.BlockSpec((d, lanes), lambda b, i: (0, 0)),
                  pl.BlockSpec((1, lanes), lambda b, i: (0, 0))],
        out_specs=pl.BlockSpec((1, ts, lanes), lambda b, i: (b, i, 0)),
        out_shape=jax.ShapeDtypeStruct((bsz, seq, lanes), F32),
        scratch_shapes=[pltpu.VMEM((1, lanes), F32)],
        compiler_params=_params(("arbitrary", "arbitrary"), blk),
        name="forget_cumsum",
    )(h3, w_pad, b_pad)
    return out[:, :, :n_heads]


def _ln_rows(v, g, b):
    mu = jnp.mean(v, axis=-1, keepdims=True)
    cen = v - mu
    var = jnp.mean(cen * cen, axis=-1, keepdims=True)
    return cen * lax.rsqrt(var + LN_EPS) * g + b


def _emit_mods(xn, mod_refs, out_refs):
    for m in range(len(out_refs)):
        sh_ref, sc_ref = mod_refs[2 * m], mod_refs[2 * m + 1]
        out_refs[m][...] = (xn * sc_ref[0] + sh_ref[0]).astype(out_refs[m].dtype)


def _ln_kernel(*refs, alpha, n_mod):
    x_ref, y_ref, gate_ref, g_ref, b_ref = refs[:5]
    mod_refs = refs[5:5 + 2 * n_mod]
    xo_ref = refs[5 + 2 * n_mod]
    h_refs = refs[6 + 2 * n_mod:]
    v = alpha * x_ref[...] + gate_ref[0] * y_ref[...].astype(F32)
    xn = _ln_rows(v, g_ref[...], b_ref[...])
    xo_ref[...] = xn
    _emit_mods(xn, mod_refs, h_refs)


def _residual_ln(x2, y2, gate1, ln_g, ln_b, mods, mod_dtypes, seq, alpha):
    t, d = x2.shape
    n_mod = len(mods)
    tm = _pick(seq, (256, 128, 64, 32, 16, 8))
    tpb = seq // tm
    tile = pl.BlockSpec((tm, d), lambda i: (i, 0))
    vec = pl.BlockSpec((1, 1, d), lambda i: (i // tpb, 0, 0))
    par = pl.BlockSpec((1, d), lambda i: (0, 0))
    in_specs = [tile, tile, vec, par, par] + [vec, vec] * n_mod
    args = [x2, y2, gate1, ln_g.reshape(1, d), ln_b.reshape(1, d)]
    for sh, sc1 in mods:
        args += [sh, sc1]
    out_shape = [jax.ShapeDtypeStruct((t, d), F32)] + \
        [jax.ShapeDtypeStruct((t, d), dt) for dt in mod_dtypes]
    blk = 2 * (_nbytes((tm, d), F32) + _nbytes((tm, d), y2.dtype) + _nbytes((tm, d), F32)
               + sum(_nbytes((tm, d), dt) for dt in mod_dtypes)) + 4 * _nbytes((tm, d), F32)
    outs = pl.pallas_call(
        functools.partial(_ln_kernel, alpha=alpha, n_mod=n_mod),
        grid=(t // tm,),
        in_specs=in_specs,
        out_specs=[tile] * (1 + n_mod),
        out_shape=out_shape,
        compiler_params=_params(("arbitrary",), blk),
        name="residual_ln",
    )(*args)
    return outs[0], list(outs[1:])


def _router_kernel(h_ref, w_ref, b_ref, idx_ref, wgt_ref, *, n_experts):
    logits = jnp.dot(h_ref[...], w_ref[...], preferred_element_type=F32,
                     precision=lax.Precision.HIGHEST) + b_ref[...]
    lane = lax.broadcasted_iota(jnp.int32, logits.shape, 1)
    logits = jnp.where(lane < n_experts, logits, -jnp.inf)
    big = jnp.int32(ROUTER_LANES)
    v1 = jnp.max(logits, axis=1, keepdims=True)
    i1 = jnp.min(jnp.where(logits == v1, lane, big), axis=1, keepdims=True)
    rest = jnp.where(lane == i1, -jnp.inf, logits)
    v2 = jnp.max(rest, axis=1, keepdims=True)
    i2 = jnp.min(jnp.where(rest == v2, lane, big), axis=1, keepdims=True)
    e2 = jnp.exp(v2 - v1)
    w1 = 1.0 / (1.0 + e2)
    w2 = e2 / (1.0 + e2)
    idx_ref[...] = jnp.where(lane == 0, i1, jnp.where(lane == 1, i2, 0))
    wgt_ref[...] = jnp.where(lane == 0, w1, jnp.where(lane == 1, w2, 0.0))


def _router(h32, w_router, b_router):
    t, d = h32.shape
    n_experts = w_router.shape[1]
    lanes = ROUTER_LANES
    w_pad = jnp.zeros((d, lanes), F32).at[:, :n_experts].set(w_router)
    b_pad = jnp.zeros((1, lanes), F32).at[0, :n_experts].set(b_router.astype(F32))
    tm = _pick(t, (256, 128, 64, 32, 16, 8))
    blk = 2 * (_nbytes((tm, d), F32) + _nbytes((d, lanes), F32) + 2 * _nbytes((tm, lanes), F32)) \
        + 4 * _nbytes((tm, d), F32)
    idx, wgt = pl.pallas_call(
        functools.partial(_router_kernel, n_experts=n_experts),
        grid=(t // tm,),
        in_specs=[pl.BlockSpec((tm, d), lambda i: (i, 0)),
                  pl.BlockSpec((d, lanes), lambda i: (0, 0)),
                  pl.BlockSpec((1, lanes), lambda i: (0, 0))],
        out_specs=[pl.BlockSpec((tm, lanes), lambda i: (i, 0)),
                   pl.BlockSpec((tm, lanes), lambda i: (i, 0))],
        out_shape=[jax.ShapeDtypeStruct((t, lanes), jnp.int32),
                   jax.ShapeDtypeStruct((t, lanes), F32)],
        compiler_params=_params(("arbitrary",), blk),
        name="router",
    )(h32, w_pad, b_pad)
    return idx[:, :TOP_K], wgt[:, :TOP_K]


def _route_plan(top_idx, n_experts, tile):
    t = top_idx.shape[0]
    p = TOP_K * t
    e_flat = top_idx.reshape(p)
    onehot = (e_flat[:, None] == jnp.arange(n_experts, dtype=jnp.int32)[None, :]).astype(jnp.int32)
    csum = jnp.cumsum(onehot, axis=0)
    counts = csum[-1]
    rank = jnp.sum(csum * onehot, axis=1) - 1
    padded = ((counts + tile - 1) // tile) * tile
    gend = jnp.cumsum(padded)
    gstart = gend - padded
    pos = jnp.sum(onehot * gstart[None, :], axis=1) + rank
    n_rows = ((p + tile - 1) // tile + n_experts) * tile
    row_token = jnp.zeros((n_rows,), jnp.int32).at[pos].set(
        jnp.arange(p, dtype=jnp.int32) // TOP_K)
    tile_start = jnp.arange(n_rows // tile, dtype=jnp.int32) * tile
    tile_expert = jnp.sum((tile_start[:, None] >= gend[None, :]).astype(jnp.int32), axis=1)
    tile_valid = (tile_start < gend[-1]).astype(jnp.int32)
    tile_expert = jnp.minimum(tile_expert, n_experts - 1)
    return pos.astype(jnp.int32), row_token, gend[-1:].astype(jnp.int32), tile_expert, tile_valid


def _row_copy(src_ref, src_row, dst_ref, dst_row, sem):
    return pltpu.make_async_copy(src_ref.at[pl.ds(src_row, 1)], dst_ref.at[pl.ds(dst_row, 1)], sem)


GATHER_UNROLL = 8


def _prefetched_gather(issue, n_steps):
    i = pl.program_id(0)
    slot = i % 2

    @pl.when(i == 0)
    def _():
        issue(i, slot)

    @pl.when(i + 1 < n_steps)
    def _():
        issue(i + 1, 1 - slot)

    return slot


def _dispatch_kernel(tok_ref, total_ref, src_ref, o_ref, buf_ref, sems, *, rows, n_steps):
    def issue(step, slot):
        base = step * rows

        @pl.when(base < total_ref[0])
        def _():
            def body(r, c):
                _row_copy(src_ref, tok_ref[base + r], buf_ref.at[slot], r, sems.at[slot]).start()
                return c

            lax.fori_loop(0, rows, body, 0, unroll=min(GATHER_UNROLL, rows))

    slot = _prefetched_gather(issue, n_steps)
    used = pl.program_id(0) * rows < total_ref[0]

    @pl.when(used)
    def _():
        def drain(r, c):
            _row_copy(src_ref, 0, buf_ref.at[slot], r, sems.at[slot]).wait()
            return c

        lax.fori_loop(0, rows, drain, 0, unroll=min(GATHER_UNROLL, rows))
        o_ref[...] = buf_ref[slot].astype(o_ref.dtype)

    @pl.when(jnp.logical_not(used))
    def _():
        o_ref[...] = jnp.zeros_like(o_ref)


def _dispatch(h32, row_token, total_rows, rows):
    t, d = h32.shape
    n_rows = row_token.shape[0]
    n_steps = n_rows // rows
    blk = 2 * _nbytes((rows, d), BF16) + 3 * _nbytes((rows, d), F32)
    grid_spec = pltpu.PrefetchScalarGridSpec(
        num_scalar_prefetch=2,
        grid=(n_steps,),
        in_specs=[pl.BlockSpec(memory_space=pl.ANY)],
        out_specs=pl.BlockSpec((rows, d), lambda i, tok, total: (i, 0)),
        scratch_shapes=[pltpu.VMEM((2, rows, d), F32), pltpu.SemaphoreType.DMA((2,))],
    )
    return pl.pallas_call(
        functools.partial(_dispatch_kernel, rows=rows, n_steps=n_steps),
        grid_spec=grid_spec,
        out_shape=jax.ShapeDtypeStruct((n_rows, d), BF16),
        compiler_params=_params(("arbitrary",), blk),
        name="expert_dispatch",
    )(row_token, total_rows, h32)


def _combine_ln_kernel(*refs, alpha, n_mod, tm, n_steps):
    pos_ref, x_ref, rows_ref, wgt_ref, gate_ref, g_ref, b_ref = refs[:7]
    mod_refs = refs[7:7 + 2 * n_mod]
    xo_ref = refs[7 + 2 * n_mod]
    h_refs = refs[8 + 2 * n_mod:8 + 3 * n_mod]
    buf_ref, sems = refs[8 + 3 * n_mod:]

    def issue(step, slot):
        base = step * tm

        def body(r, c):
            for k in range(TOP_K):
                _row_copy(rows_ref, pos_ref[(base + r) * TOP_K + k], buf_ref.at[slot, k], r,
                          sems.at[slot]).start()
            return c

        lax.fori_loop(0, tm, body, 0, unroll=min(GATHER_UNROLL, tm))

    slot = _prefetched_gather(issue, n_steps)

    def drain(r, c):
        for k in range(TOP_K):
            _row_copy(rows_ref, 0, buf_ref.at[slot, k], r, sems.at[slot]).wait()
        return c

    lax.fori_loop(0, tm, drain, 0, unroll=min(GATHER_UNROLL, tm))
    wgt = wgt_ref[...]
    y = wgt[:, 0:1] * buf_ref[slot, 0]
    for k in range(1, TOP_K):
        y = y + wgt[:, k:k + 1] * buf_ref[slot, k]
    v = alpha * x_ref[...] + gate_ref[0] * y
    xn = _ln_rows(v, g_ref[...], b_ref[...])
    xo_ref[...] = xn
    _emit_mods(xn, mod_refs, h_refs)


def _combine_ln(x2, expert_rows, pos, top_w, gate1, ln_g, ln_b, mods, mod_dtypes, seq, alpha):
    t, d = x2.shape
    n_mod = len(mods)
    tm = _pick(seq, (128, 64, 32, 16, 8))
    tpb = seq // tm
    n_steps = t // tm
    tile = pl.BlockSpec((tm, d), lambda i, pos: (i, 0))
    vec = pl.BlockSpec((1, 1, d), lambda i, pos: (i // tpb, 0, 0))
    par = pl.BlockSpec((1, d), lambda i, pos: (0, 0))
    wgt = pl.BlockSpec((tm, TOP_K), lambda i, pos: (i, 0))
    in_specs = [tile, pl.BlockSpec(memory_space=pl.ANY), wgt, vec, par, par] + [vec, vec] * n_mod
    args = [x2, expert_rows, top_w, gate1, ln_g.reshape(1, d), ln_b.reshape(1, d)]
    for sh, sc1 in mods:
        args += [sh, sc1]
    out_shape = [jax.ShapeDtypeStruct((t, d), F32)] + \
        [jax.ShapeDtypeStruct((t, d), dt) for dt in mod_dtypes]
    blk = 2 * (2 * _nbytes((tm, d), F32) + sum(_nbytes((tm, d), dt) for dt in mod_dtypes)) \
        + (2 * TOP_K + 4) * _nbytes((tm, d), F32)
    grid_spec = pltpu.PrefetchScalarGridSpec(
        num_scalar_prefetch=1,
        grid=(n_steps,),
        in_specs=in_specs,
        out_specs=[tile] * (1 + n_mod),
        scratch_shapes=[pltpu.VMEM((2, TOP_K, tm, d), F32), pltpu.SemaphoreType.DMA((2,))],
    )
    outs = pl.pallas_call(
        functools.partial(_combine_ln_kernel, alpha=alpha, n_mod=n_mod, tm=tm, n_steps=n_steps),
        grid_spec=grid_spec,
        out_shape=out_shape,
        compiler_params=_params(("arbitrary",), blk),
        name="combine_ln",
    )(pos, *args)
    return outs[0], list(outs[1:])


def kernel(x, c, ada_w, ada_b, ada_table, kv_table, a_w_qkv, a_w_o, kv_w, kv_b_f, b_w_q, b_w_o,
           ln_g, ln_b, ffn_w_up, ffn_w_down, moe_w_router, moe_b_router, moe_w_up, moe_w_down):
    bsz, seq, d = x.shape
    t = bsz * seq
    depth, n_mod = ada_table.shape[0], ada_table.shape[1]
    n_a = a_w_qkv.shape[0]
    n_heads = kv_w.shape[1] - 2 * d
    head_dim = d // n_heads
    n_experts = moe_w_router.shape[-1]
    alpha = (2.0 * depth) ** 0.25
    expert_tile = _pick(TOP_K * t, (512, 256, 128, 64, 32, 16))
    q_scale = head_dim ** -0.5 * LOG2E
    heads_per_step = ATTN_HEADS_PER_STEP if n_heads % ATTN_HEADS_PER_STEP == 0 else 1

    mod = _ada_mod(c, ada_w, ada_b).reshape(bsz, n_mod, d)

    def vec3(v):
        return v.reshape(bsz, 1, d)

    def layer_mods(l):
        m = mod + ada_table[l][None]
        return [m[:, i] for i in range(n_mod)]

    def mix_mod(l):
        m = layer_mods(l)
        return vec3(m[0]), vec3(1.0 + m[1])

    kv_mod = (vec3(mod[:, 0] + kv_table[0]), vec3(1.0 + mod[:, 1] + kv_table[1]))

    x2 = x.reshape(t, d)
    h = _modcast(x2, *mix_mod(0), seq, BF16)
    h_kv = None
    shared = None
    for l in range(depth):
        m = layer_mods(l)
        gate_mix1, gate_ffn1 = vec3(1.0 + m[2]), vec3(1.0 + m[5])
        ffn_mod = (vec3(m[3]), vec3(1.0 + m[4]))
        is_moe = l % 2 == 1

        if l < n_a:
            qkv = _matmul(h, a_w_qkv, l, 0, 3 * d, BF16, scaled_cols=d, col_scale=-q_scale)
            o = _stick_attention(qkv.reshape(bsz, seq, 3 * d), n_heads, head_dim, BF16,
                                 heads_per_step)
            mix = _matmul(o.reshape(t, d), a_w_o, l, 0, d, F32)
        else:
            if shared is None:
                kv = _matmul(h_kv, kv_w, None, 0, 2 * d, BF16)
                log_f = _forget_cumsum(h_kv.reshape(bsz, seq, d), kv_w[:, 2 * d:], kv_b_f)
                shared = (kv.reshape(bsz, seq, 2 * d), jnp.transpose(log_f, (0, 2, 1)))
            j = l - n_a
            q = _matmul(h, b_w_q, j, 0, d, BF16, scaled_cols=d, col_scale=q_scale)
            o = _fox_attention(q.reshape(bsz, seq, d), shared[0], shared[1],
                               n_heads, head_dim, BF16, heads_per_step)
            mix = _matmul(o.reshape(t, d), b_w_o, j, 0, d, F32)
        x2, (h_ffn,) = _residual_ln(x2, mix, gate_mix1, ln_g[l, 0], ln_b[l, 0], [ffn_mod],
                                    [F32 if is_moe else BF16], seq, alpha)

        next_mods, next_dtypes = [], []
        if l + 1 < depth:
            next_mods.append(mix_mod(l + 1))
            next_dtypes.append(BF16)
            if l + 1 == n_a:
                next_mods.append(kv_mod)
                next_dtypes.append(BF16)
        if not is_moe:
            act = _swiglu_up(h_ffn, ffn_w_up, l // 2)
            y = _matmul(act, ffn_w_down, l // 2, 0, d, F32)
            x2, hs = _residual_ln(x2, y, gate_ffn1, ln_g[l, 1], ln_b[l, 1], next_mods,
                                  next_dtypes, seq, alpha)
        else:
            e = l // 2
            top_idx, top_w = _router(h_ffn, moe_w_router[e], moe_b_router[e])
            pos, row_token, total_rows, tile_expert, tile_valid = _route_plan(
                top_idx, n_experts, expert_tile)
            hg = _dispatch(h_ffn, row_token, total_rows, min(expert_tile, 256))
            act = _grouped_swiglu_up(hg, moe_w_up, e, tile_expert, tile_valid, expert_tile)
            rows = _grouped_matmul(act, moe_w_down, e, tile_expert, tile_valid, expert_tile, F32)
            x2, hs = _combine_ln(x2, rows, pos, top_w, gate_ffn1, ln_g[l, 1], ln_b[l, 1],
                                 next_mods, next_dtypes, seq, alpha)
        if hs:
            h = hs[0]
            if len(hs) > 1:
                h_kv = hs[1]
    return x2.reshape(bsz, seq, d)
```

```python
import functools

import jax
import jax.numpy as jnp
from jax import lax
from jax.experimental import pallas as pl
from jax.experimental.pallas import tpu as pltpu

F32 = jnp.float32
BF16 = jnp.bfloat16

LN_EPS = 1e-5
TOP_K = 2
NEG_INF = -1e30
LOG2E = 1.4426950408889634
LANE = 128
FORGET_LANES = LANE
ROUTER_LANES = LANE
BRANCH_DTYPE = BF16
STICK_HEADS_PER_STEP = 8
FOX_HEADS_PER_STEP = 4

V7X_VMEM_BYTES = 64 * 1024 * 1024
VMEM_HEADROOM_BYTES = 12 * 1024 * 1024
VMEM_CAP_BYTES = V7X_VMEM_BYTES - 6 * 1024 * 1024


def _params(semantics, block_bytes):
    limit = min(int(block_bytes) + VMEM_HEADROOM_BYTES, VMEM_CAP_BYTES)
    return pltpu.CompilerParams(dimension_semantics=semantics, vmem_limit_bytes=limit)


def _nbytes(shape, dtype):
    n = 1
    for s in shape:
        n *= s
    return n * jnp.dtype(dtype).itemsize


def _pick(n, prefs):
    for p in prefs:
        if n % p == 0:
            return p
    return n


def _ada_kernel(c_ref, w_ref, b_ref, o_ref):
    c = c_ref[...]
    s = c * jax.nn.sigmoid(c)
    o_ref[...] = jnp.dot(s.astype(BF16), w_ref[...].astype(BF16),
                         preferred_element_type=F32) + b_ref[...]


def _ada_mod(c, ada_w, ada_b):
    bsz, d = c.shape
    n = ada_w.shape[1]
    rows = 8
    c8 = jnp.zeros((rows, d), F32).at[:bsz].set(c)
    tn = _pick(n, (512, 256, 128))
    blk = 2 * (_nbytes((rows, d), F32) + _nbytes((d, tn), F32) + 2 * _nbytes((rows, tn), F32))
    out = pl.pallas_call(
        _ada_kernel,
        grid=(n // tn,),
        in_specs=[pl.BlockSpec((rows, d), lambda j: (0, 0)),
                  pl.BlockSpec((d, tn), lambda j: (0, j)),
                  pl.BlockSpec((1, tn), lambda j: (0, j))],
        out_specs=pl.BlockSpec((rows, tn), lambda j: (0, j)),
        out_shape=jax.ShapeDtypeStruct((rows, n), F32),
        compiler_params=_params(("arbitrary",), blk),
        name="ada_mod",
    )(c8, ada_w, ada_b.reshape(1, n))
    return out[:bsz]


def _modcast_kernel(x_ref, sh_ref, sc_ref, o_ref):
    o_ref[...] = (x_ref[...] * sc_ref[0] + sh_ref[0]).astype(o_ref.dtype)


def _modcast(x2, shift, scale1, seq, out_dtype):
    t, d = x2.shape
    tm = _pick(seq, (256, 128, 64, 32, 16, 8))
    tpb = seq // tm
    vec = pl.BlockSpec((1, 1, d), lambda i: (i // tpb, 0, 0))
    blk = 2 * (_nbytes((tm, d), F32) + _nbytes((tm, d), out_dtype))
    return pl.pallas_call(
        _modcast_kernel,
        grid=(t // tm,),
        in_specs=[pl.BlockSpec((tm, d), lambda i: (i, 0)), vec, vec],
        out_specs=pl.BlockSpec((tm, d), lambda i: (i, 0)),
        out_shape=jax.ShapeDtypeStruct((t, d), out_dtype),
        compiler_params=_params(("arbitrary",), blk),
        name="modcast",
    )(x2, shift, scale1)


def _mm_kernel(a_ref, w_ref, o_ref, wb_ref, *, scaled_blocks, col_scale):
    @pl.when(pl.program_id(1) == 0)
    def _():
        wb_ref[...] = w_ref[...].astype(BF16)

    res = jnp.dot(a_ref[...], wb_ref[...], preferred_element_type=F32)
    if scaled_blocks:
        res = res * jnp.where(pl.program_id(0) < scaled_blocks, col_scale, 1.0)
    o_ref[...] = res.astype(o_ref.dtype)


MM_BLOCK_ELEMS = 4 * 1024 * 1024


def _mm_tiles(m, k, n):
    tn = _pick(n, (512, 256, 128))
    tm = _pick(m, tuple(t for t in (1024, 512, 256, 128, 64, 32, 16, 8) if t * k <= MM_BLOCK_ELEMS))
    w_buffers = 2 if k * tn <= MM_BLOCK_ELEMS // 2 else 1
    return tm, tn, w_buffers


def _matmul(a, w, w_index, col0, n, out_dtype, scaled_cols=0, col_scale=1.0):
    m, k = a.shape
    tm, tn, w_buffers = _mm_tiles(m, k, n)
    cb0 = col0 // tn
    assert scaled_cols % tn == 0
    mode = pl.Buffered(w_buffers)
    if w_index is None:
        w_spec = pl.BlockSpec((k, tn), lambda j, i: (0, cb0 + j), pipeline_mode=mode)
    else:
        w_spec = pl.BlockSpec((None, k, tn), lambda j, i: (w_index, 0, cb0 + j),
                              pipeline_mode=mode)
    blk = (2 * (_nbytes((tm, k), BF16) + _nbytes((tm, tn), out_dtype))
           + w_buffers * _nbytes((k, tn), F32) + _nbytes((k, tn), BF16))
    return pl.pallas_call(
        functools.partial(_mm_kernel, scaled_blocks=scaled_cols // tn, col_scale=col_scale),
        grid=(n // tn, m // tm),
        in_specs=[pl.BlockSpec((tm, k), lambda j, i: (i, 0)), w_spec],
        out_specs=pl.BlockSpec((tm, tn), lambda j, i: (i, j)),
        out_shape=jax.ShapeDtypeStruct((m, n), out_dtype),
        scratch_shapes=[pltpu.VMEM((k, tn), BF16)],
        compiler_params=_params(("arbitrary", "arbitrary"), blk),
        name="matmul",
    )(a, w)


def _swiglu_kernel(a_ref, wg_ref, wu_ref, o_ref, wgb_ref, wub_ref):
    @pl.when(pl.program_id(1) == 0)
    def _():
        wgb_ref[...] = wg_ref[...].astype(BF16)
        wub_ref[...] = wu_ref[...].astype(BF16)

    a = a_ref[...]
    g = jnp.dot(a, wgb_ref[...], preferred_element_type=F32)
    u = jnp.dot(a, wub_ref[...], preferred_element_type=F32)
    o_ref[...] = (g * jax.nn.sigmoid(g) * u).astype(o_ref.dtype)


def _swiglu_up(a, w_up, w_index):
    m, k = a.shape
    f = w_up.shape[-1] // 2
    tn = _pick(f, (256, 128))
    tm = _pick(m, (1024, 512, 256, 128, 64, 32, 16, 8))
    nb = f // tn
    blk = (2 * (_nbytes((tm, k), BF16) + 2 * _nbytes((k, tn), F32) + _nbytes((tm, tn), BF16))
           + 2 * _nbytes((k, tn), BF16))
    return pl.pallas_call(
        _swiglu_kernel,
        grid=(nb, m // tm),
        in_specs=[pl.BlockSpec((tm, k), lambda j, i: (i, 0)),
                  pl.BlockSpec((None, k, tn), lambda j, i: (w_index, 0, j)),
                  pl.BlockSpec((None, k, tn), lambda j, i: (w_index, 0, nb + j))],
        out_specs=pl.BlockSpec((tm, tn), lambda j, i: (i, j)),
        out_shape=jax.ShapeDtypeStruct((m, f), BF16),
        scratch_shapes=[pltpu.VMEM((k, tn), BF16), pltpu.VMEM((k, tn), BF16)],
        compiler_params=_params(("arbitrary", "arbitrary"), blk),
        name="swiglu_up",
    )(a, w_up, w_up)


def _expert_changed(te_ref, i):
    prev = te_ref[jnp.maximum(i - 1, 0)]
    return jnp.logical_or(i == 0, te_ref[i] != prev)


def _gswiglu_kernel(te_ref, tv_ref, a_ref, wg_ref, wu_ref, o_ref, wgb_ref, wub_ref):
    i = pl.program_id(1)

    @pl.when(_expert_changed(te_ref, i))
    def _():
        wgb_ref[...] = wg_ref[...].astype(BF16)
        wub_ref[...] = wu_ref[...].astype(BF16)

    @pl.when(tv_ref[i] > 0)
    def _():
        a = a_ref[...]
        g = jnp.dot(a, wgb_ref[...], preferred_element_type=F32)
        u = jnp.dot(a, wub_ref[...], preferred_element_type=F32)
        o_ref[...] = (g * jax.nn.sigmoid(g) * u).astype(o_ref.dtype)

    @pl.when(tv_ref[i] == 0)
    def _():
        o_ref[...] = jnp.zeros_like(o_ref)


def _grouped_swiglu_up(a, w_up, w_index, tile_expert, tile_valid, tile):
    p, k = a.shape
    f = w_up.shape[-1] // 2
    tn = _pick(f, (256, 128))
    nb = f // tn
    blk = (2 * (_nbytes((tile, k), BF16) + 2 * _nbytes((k, tn), F32) + _nbytes((tile, tn), BF16))
           + 2 * _nbytes((k, tn), BF16))
    grid_spec = pltpu.PrefetchScalarGridSpec(
        num_scalar_prefetch=2,
        grid=(nb, p // tile),
        in_specs=[pl.BlockSpec((tile, k), lambda j, i, te, tv: (i, 0)),
                  pl.BlockSpec((None, None, k, tn), lambda j, i, te, tv: (w_index, te[i], 0, j)),
                  pl.BlockSpec((None, None, k, tn),
                               lambda j, i, te, tv: (w_index, te[i], 0, nb + j))],
        out_specs=pl.BlockSpec((tile, tn), lambda j, i, te, tv: (i, j)),
        scratch_shapes=[pltpu.VMEM((k, tn), BF16), pltpu.VMEM((k, tn), BF16)],
    )
    return pl.pallas_call(
        _gswiglu_kernel,
        grid_spec=grid_spec,
        out_shape=jax.ShapeDtypeStruct((p, f), BF16),
        compiler_params=_params(("arbitrary", "arbitrary"), blk),
        name="expert_swiglu_up",
    )(tile_expert, tile_valid, a, w_up, w_up)


def _gmm_kernel(te_ref, tv_ref, a_ref, w_ref, o_ref, wb_ref):
    i = pl.program_id(1)

    @pl.when(_expert_changed(te_ref, i))
    def _():
        wb_ref[...] = w_ref[...].astype(BF16)

    @pl.when(tv_ref[i] > 0)
    def _():
        o_ref[...] = jnp.dot(a_ref[...], wb_ref[...],
                             preferred_element_type=F32).astype(o_ref.dtype)

    @pl.when(tv_ref[i] == 0)
    def _():
        o_ref[...] = jnp.zeros_like(o_ref)


def _grouped_matmul(a, w, w_index, tile_expert, tile_valid, tile, out_dtype):
    p, k = a.shape
    n = w.shape[-1]
    tn = _pick(n, tuple(c for c in (1024, 512, 256, 128) if k * c <= MM_BLOCK_ELEMS // 2))
    blk = (2 * (_nbytes((tile, k), BF16) + _nbytes((k, tn), F32) + _nbytes((tile, tn), out_dtype))
           + _nbytes((k, tn), BF16))
    grid_spec = pltpu.PrefetchScalarGridSpec(
        num_scalar_prefetch=2,
        grid=(n // tn, p // tile),
        in_specs=[pl.BlockSpec((tile, k), lambda j, i, te, tv: (i, 0)),
                  pl.BlockSpec((None, None, k, tn), lambda j, i, te, tv: (w_index, te[i], 0, j))],
        out_specs=pl.BlockSpec((tile, tn), lambda j, i, te, tv: (i, j)),
        scratch_shapes=[pltpu.VMEM((k, tn), BF16)],
    )
    return pl.pallas_call(
        _gmm_kernel,
        grid_spec=grid_spec,
        out_shape=jax.ShapeDtypeStruct((p, n), out_dtype),
        compiler_params=_params(("arbitrary", "arbitrary"), blk),
        name="expert_matmul",
    )(tile_expert, tile_valid, a, w)


def _softplus(z):
    return jnp.maximum(z, 0.0) + jnp.log(1.0 + jnp.exp(-jnp.abs(z)))


def _neg_abs(x):
    bits = lax.bitcast_convert_type(x, jnp.uint32) | jnp.uint32(0x80000000)
    return lax.bitcast_convert_type(bits, F32)


def _lanes(x, n):
    reps = n // x.shape[1]
    return x if reps == 1 else jnp.concatenate([x] * reps, axis=1)


def _stick_kernel(q_ref, k_ref, v_ref, o_ref, acc_ref, carry_ref, *, tq, tk, heads, hd):
    qi = pl.program_id(2)
    n_sub = tq // tk
    row = lax.broadcasted_iota(jnp.int32, (tk, tk), 0)
    col = lax.broadcasted_iota(jnp.int32, (tk, tk), 1)
    later = (row > col).astype(BF16)
    acc_ref[...] = jnp.zeros_like(acc_ref)
    carry_ref[...] = jnp.zeros_like(carry_ref)

    def block(g, r0, k0, diag_off):
        nr = tq - r0
        lanes = slice(g * hd, (g + 1) * hd)
        q = q_ref[0, r0:tq, lanes]
        k = k_ref[0, pl.ds(k0, tk), lanes]
        v = v_ref[0, pl.ds(k0, tk), lanes]
        s = lax.dot_general(q, k, (((1,), (1,)), ((), ())), preferred_element_type=F32)
        soft = jnp.log(1.0 + jnp.exp2(_neg_abs(s))) * LOG2E
        log_keep = jnp.minimum(s, 0.0) - soft
        if diag_off is not None:
            q_pos = r0 + lax.broadcasted_iota(jnp.int32, (nr, tk), 0)
            k_pos = diag_off + lax.broadcasted_iota(jnp.int32, (nr, tk), 1)
            mask = k_pos < q_pos
            log_keep = jnp.where(mask, log_keep, 0.0)
        suffix = jnp.dot(log_keep.astype(BF16), later, preferred_element_type=F32)
        carry = carry_ref[g, r0:tq, :]
        w = jnp.exp2((log_keep - s) + suffix + _lanes(carry, tk))
        if diag_off is not None:
            w = jnp.where(mask, w, 0.0)
        acc_ref[g, r0:tq, :] += jnp.dot(w.astype(BF16), v, preferred_element_type=F32)
        carry_ref[g, r0:tq, :] = carry + jnp.sum(log_keep, axis=1, keepdims=True)

    diag0 = qi * tq
    for sub in range(n_sub - 1, -1, -1):
        for g in range(heads):
            block(g, sub * tk, pl.multiple_of(diag0 + sub * tk, tk), sub * tk)

    def body(j, c):
        k0 = pl.multiple_of(diag0 - (j + 1) * tk, tk)
        for g in range(heads):
            block(g, 0, k0, None)
        return c

    lax.fori_loop(0, qi * n_sub, body, 0)
    for g in range(heads):
        o_ref[0, :, g * hd:(g + 1) * hd] = acc_ref[g].astype(o_ref.dtype)


def _attn_tiles(seq):
    tq = _pick(seq, (512, 256, 128))
    tk = min(tq, 256)
    return tq, tk


def _stick_attention(qkv, n_heads, head_dim, out_dtype, heads_per_step):
    bsz, seq, _ = qkv.shape
    d = n_heads * head_dim
    g = heads_per_step
    tq, tk = _attn_tiles(seq)
    width = g * head_dim
    kern = functools.partial(_stick_kernel, tq=tq, tk=tk, heads=g, hd=head_dim)
    blk = (2 * (2 * _nbytes((tq, width), BF16) + 2 * _nbytes((seq, width), BF16))
           + g * (_nbytes((tq, head_dim), F32) + _nbytes((tq, LANE), F32))
           + 8 * g * _nbytes((tq, tk), F32))
    groups = n_heads // g
    return pl.pallas_call(
        kern,
        grid=(bsz, groups, seq // tq),
        in_specs=[pl.BlockSpec((1, tq, width), lambda b, h, i: (b, i, h)),
                  pl.BlockSpec((1, seq, width), lambda b, h, i: (b, 0, groups + h)),
                  pl.BlockSpec((1, seq, width), lambda b, h, i: (b, 0, 2 * groups + h))],
        out_specs=pl.BlockSpec((1, tq, width), lambda b, h, i: (b, i, h)),
        out_shape=jax.ShapeDtypeStruct((bsz, seq, d), out_dtype),
        scratch_shapes=[pltpu.VMEM((g, tq, head_dim), F32), pltpu.VMEM((g, tq, LANE), F32)],
        compiler_params=_params(("arbitrary", "arbitrary", "arbitrary"), blk),
        name="stick_attention",
    )(qkv, qkv, qkv)


def _fox_kernel(q_ref, k_ref, v_ref, fk_ref, o_ref, m_ref, fqb_ref, acc_ref,
                *, tq, tk, tr, heads, hd):
    qi = pl.program_id(2)
    assert tq == tk
    m_ref[...] = jnp.full_like(m_ref, NEG_INF)
    acc_ref[...] = jnp.zeros_like(acc_ref)
    for g in range(heads):
        f_row = fk_ref[0, g, pl.ds(qi, 1), :]
        for c in range(tk // LANE):
            chunk = f_row[:, c * LANE:(c + 1) * LANE]
            fqb_ref[g, c * LANE:(c + 1) * LANE, :] = jnp.broadcast_to(chunk, (LANE, LANE)).T

    def block(g, r0, r1, kb, nk, masked):
        nr = r1 - r0
        lanes = slice(g * hd, (g + 1) * hd)
        k0 = pl.multiple_of(kb * tk, tk)
        q = q_ref[0, r0:r1, lanes]
        k = k_ref[0, pl.ds(k0, nk), lanes]
        v1 = jnp.concatenate([v_ref[0, pl.ds(k0, nk), lanes], jnp.ones((nk, LANE), BF16)], axis=1)
        fk = fk_ref[0, g, pl.ds(kb, 1), :][:, :nk]
        u = lax.dot_general(q, k, (((1,), (1,)), ((), ())), preferred_element_type=F32) - fk
        if masked:
            q_pos = r0 + lax.broadcasted_iota(jnp.int32, (nr, nk), 0)
            k_pos = lax.broadcasted_iota(jnp.int32, (nr, nk), 1)
            u = jnp.where(k_pos <= q_pos, u, NEG_INF)
        fq = fqb_ref[g, r0:r1, :]
        m_prev = m_ref[g, r0:r1, :]
        m_new = jnp.maximum(m_prev, jnp.max(u, axis=1, keepdims=True) + fq)
        alpha = jnp.exp2(m_prev - m_new)
        p = jnp.exp2(u + _lanes(fq - m_new, nk))
        pv = jnp.dot(p.astype(BF16), v1, preferred_element_type=F32)
        acc_ref[g, r0:r1, :] = _lanes(alpha, hd + LANE) * acc_ref[g, r0:r1, :] + pv
        m_ref[g, r0:r1, :] = m_new

    def body(j, c):
        for g in range(heads):
            block(g, 0, tq, j, tk, False)
        return c

    lax.fori_loop(0, qi, body, 0)
    for r0 in range(0, tq, tr):
        for g in range(heads):
            block(g, r0, r0 + tr, qi, r0 + tr, True)
    for g in range(heads):
        acc = acc_ref[g]
        o_ref[0, :, g * hd:(g + 1) * hd] = (acc[:, :hd] / acc[:, hd:]).astype(o_ref.dtype)


def _fox_attention(q, kv, log_f_cum, n_heads, head_dim, out_dtype, heads_per_step):
    bsz, seq, d = q.shape
    g = heads_per_step
    assert head_dim == LANE
    tq = tk = _pick(seq, (512, 256, 128))
    tr = min(tq, 256)
    width = g * head_dim
    groups = n_heads // g
    fk = log_f_cum.reshape(bsz, n_heads, seq // tk, tk)
    kern = functools.partial(_fox_kernel, tq=tq, tk=tk, tr=tr, heads=g, hd=head_dim)
    blk = (2 * (2 * _nbytes((tq, width), BF16) + 2 * _nbytes((seq, width), BF16)
                + g * _nbytes((seq // tk, tk), F32))
           + 4 * g * _nbytes((tq, LANE), F32) + 8 * g * _nbytes((tq, tk), F32))
    return pl.pallas_call(
        kern,
        grid=(bsz, groups, seq // tq),
        in_specs=[pl.BlockSpec((1, tq, width), lambda b, h, i: (b, i, h)),
                  pl.BlockSpec((1, seq, width), lambda b, h, i: (b, 0, h)),
                  pl.BlockSpec((1, seq, width), lambda b, h, i: (b, 0, groups + h)),
                  pl.BlockSpec((1, g, seq // tk, tk), lambda b, h, i: (b, h, 0, 0))],
        out_specs=pl.BlockSpec((1, tq, width), lambda b, h, i: (b, i, h)),
        out_shape=jax.ShapeDtypeStruct((bsz, seq, d), out_dtype),
        scratch_shapes=[pltpu.VMEM((g, tq, LANE), F32), pltpu.VMEM((g, tq, LANE), F32),
                        pltpu.VMEM((g, tq, head_dim + LANE), F32)],
        compiler_params=_params(("arbitrary", "arbitrary", "arbitrary"), blk),
        name="fox_attention",
    )(q, kv, kv, fk)


def _forget_kernel(h_ref, w_ref, b_ref, o_ref, carry_ref, *, ts):
    @pl.when(pl.program_id(1) == 0)
    def _():
        carry_ref[...] = jnp.zeros_like(carry_ref)

    logit = jnp.dot(h_ref[0], w_ref[...].astype(BF16), preferred_element_type=F32) + b_ref[...]
    log_f = -_softplus(-logit) * LOG2E
    row = lax.broadcasted_iota(jnp.int32, (ts, ts), 0)
    col = lax.broadcasted_iota(jnp.int32, (ts, ts), 1)
    upto = (col <= row).astype(F32)
    csum = jnp.dot(upto, log_f, preferred_element_type=F32,
                   precision=lax.Precision.HIGHEST) + carry_ref[...]
    o_ref[0] = csum
    carry_ref[...] = csum[ts - 1:ts, :]


def _forget_cumsum(h3, w_f, b_f):
    bsz, seq, d = h3.shape
    n_heads = w_f.shape[1]
    lanes = FORGET_LANES
    w_pad = jnp.zeros((d, lanes), F32).at[:, :n_heads].set(w_f)
    b_pad = jnp.zeros((1, lanes), F32).at[0, :n_heads].set(b_f.astype(F32))
    ts = _pick(seq, (256, 128, 64, 32, 16, 8))
    blk = 2 * (_nbytes((ts, d), BF16) + _nbytes((d, lanes), F32) + _nbytes((ts, lanes), F32)) \
        + 4 * _nbytes((ts, ts), F32)
    out = pl.pallas_call(
        functools.partial(_forget_kernel, ts=ts),
        grid=(bsz, seq // ts),
        in_specs=[pl.BlockSpec((1, ts, d), lambda b, i: (b, i, 0)),
                  pl.BlockSpec((d, lanes), lambda b, i: (0, 0)),
                  pl.BlockSpec((1, lanes), lambda b, i: (0, 0))],
        out_specs=pl.BlockSpec((1, ts, lanes), lambda b, i: (b, i, 0)),
        out_shape=jax.ShapeDtypeStruct((bsz, seq, lanes), F32),
        scratch_shapes=[pltpu.VMEM((1, lanes), F32)],
        compiler_params=_params(("arbitrary", "arbitrary"), blk),
        name="forget_cumsum",
    )(h3, w_pad, b_pad)
    return out[:, :, :n_heads]


def _ln_rows(v, g, b):
    mu = jnp.mean(v, axis=-1, keepdims=True)
    cen = v - mu
    var = jnp.mean(cen * cen, axis=-1, keepdims=True)
    return cen * lax.rsqrt(var + LN_EPS) * g + b


def _emit_mods(xn, mod_refs, out_refs):
    for m in range(len(out_refs)):
        sh_ref, sc_ref = mod_refs[2 * m], mod_refs[2 * m + 1]
        out_refs[m][...] = (xn * sc_ref[0] + sh_ref[0]).astype(out_refs[m].dtype)


def _ln_kernel(*refs, alpha, n_mod):
    x_ref, y_ref, gate_ref, g_ref, b_ref = refs[:5]
    mod_refs = refs[5:5 + 2 * n_mod]
    xo_ref = refs[5 + 2 * n_mod]
    h_refs = refs[6 + 2 * n_mod:]
    v = alpha * x_ref[...] + gate_ref[0] * y_ref[...].astype(F32)
    xn = _ln_rows(v, g_ref[...], b_ref[...])
    xo_ref[...] = xn
    _emit_mods(xn, mod_refs, h_refs)


def _residual_ln(x2, y2, gate1, ln_g, ln_b, mods, mod_dtypes, seq, alpha):
    t, d = x2.shape
    n_mod = len(mods)
    tm = _pick(seq, (256, 128, 64, 32, 16, 8))
    tpb = seq // tm
    tile = pl.BlockSpec((tm, d), lambda i: (i, 0))
    vec = pl.BlockSpec((1, 1, d), lambda i: (i // tpb, 0, 0))
    par = pl.BlockSpec((1, d), lambda i: (0, 0))
    in_specs = [tile, tile, vec, par, par] + [vec, vec] * n_mod
    args = [x2, y2, gate1, ln_g.reshape(1, d), ln_b.reshape(1, d)]
    for sh, sc1 in mods:
        args += [sh, sc1]
    out_shape = [jax.ShapeDtypeStruct((t, d), F32)] + \
        [jax.ShapeDtypeStruct((t, d), dt) for dt in mod_dtypes]
    blk = 2 * (_nbytes((tm, d), F32) + _nbytes((tm, d), y2.dtype) + _nbytes((tm, d), F32)
               + sum(_nbytes((tm, d), dt) for dt in mod_dtypes)) + 4 * _nbytes((tm, d), F32)
    outs = pl.pallas_call(
        functools.partial(_ln_kernel, alpha=alpha, n_mod=n_mod),
        grid=(t // tm,),
        in_specs=in_specs,
        out_specs=[tile] * (1 + n_mod),
        out_shape=out_shape,
        compiler_params=_params(("arbitrary",), blk),
        name="residual_ln",
    )(*args)
    return outs[0], list(outs[1:])


def _router_kernel(h_ref, w_ref, b_ref, idx_ref, wgt_ref, *, n_experts):
    logits = jnp.dot(h_ref[...], w_ref[...], preferred_element_type=F32,
                     precision=lax.Precision.HIGHEST) + b_ref[...]
    lane = lax.broadcasted_iota(jnp.int32, logits.shape, 1)
    logits = jnp.where(lane < n_experts, logits, -jnp.inf)
    big = jnp.int32(ROUTER_LANES)
    v1 = jnp.max(logits, axis=1, keepdims=True)
    i1 = jnp.min(jnp.where(logits == v1, lane, big), axis=1, keepdims=True)
    rest = jnp.where(lane == i1, -jnp.inf, logits)
    v2 = jnp.max(rest, axis=1, keepdims=True)
    i2 = jnp.min(jnp.where(rest == v2, lane, big), axis=1, keepdims=True)
    e2 = jnp.exp(v2 - v1)
    w1 = 1.0 / (1.0 + e2)
    w2 = e2 / (1.0 + e2)
    idx_ref[...] = jnp.where(lane == 0, i1, jnp.where(lane == 1, i2, 0))
    wgt_ref[...] = jnp.where(lane == 0, w1, jnp.where(lane == 1, w2, 0.0))


def _router(h32, w_router, b_router):
    t, d = h32.shape
    n_experts = w_router.shape[1]
    lanes = ROUTER_LANES
    w_pad = jnp.zeros((d, lanes), F32).at[:, :n_experts].set(w_router)
    b_pad = jnp.zeros((1, lanes), F32).at[0, :n_experts].set(b_router.astype(F32))
    tm = _pick(t, (256, 128, 64, 32, 16, 8))
    blk = 2 * (_nbytes((tm, d), F32) + _nbytes((d, lanes), F32) + 2 * _nbytes((tm, lanes), F32)) \
        + 4 * _nbytes((tm, d), F32)
    idx, wgt = pl.pallas_call(
        functools.partial(_router_kernel, n_experts=n_experts),
        grid=(t // tm,),
        in_specs=[pl.BlockSpec((tm, d), lambda i: (i, 0)),
                  pl.BlockSpec((d, lanes), lambda i: (0, 0)),
                  pl.BlockSpec((1, lanes), lambda i: (0, 0))],
        out_specs=[pl.BlockSpec((tm, lanes), lambda i: (i, 0)),
                   pl.BlockSpec((tm, lanes), lambda i: (i, 0))],
        out_shape=[jax.ShapeDtypeStruct((t, lanes), jnp.int32),
                   jax.ShapeDtypeStruct((t, lanes), F32)],
        compiler_params=_params(("arbitrary",), blk),
        name="router",
    )(h32, w_pad, b_pad)
    return idx[:, :TOP_K], wgt[:, :TOP_K]


def _route_plan(top_idx, n_experts, tile):
    t = top_idx.shape[0]
    p = TOP_K * t
    e_flat = top_idx.reshape(p)
    onehot = (e_flat[:, None] == jnp.arange(n_experts, dtype=jnp.int32)[None, :]).astype(jnp.int32)
    csum = jnp.cumsum(onehot, axis=0)
    counts = csum[-1]
    rank = jnp.sum(csum * onehot, axis=1) - 1
    padded = ((counts + tile - 1) // tile) * tile
    gend = jnp.cumsum(padded)
    gstart = gend - padded
    pos = jnp.sum(onehot * gstart[None, :], axis=1) + rank
    n_rows = ((p + tile - 1) // tile + n_experts) * tile
    row_token = jnp.zeros((n_rows,), jnp.int32).at[pos].set(
        jnp.arange(p, dtype=jnp.int32) // TOP_K)
    tile_start = jnp.arange(n_rows // tile, dtype=jnp.int32) * tile
    tile_expert = jnp.sum((tile_start[:, None] >= gend[None, :]).astype(jnp.int32), axis=1)
    tile_valid = (tile_start < gend[-1]).astype(jnp.int32)
    tile_expert = jnp.minimum(tile_expert, n_experts - 1)
    return pos.astype(jnp.int32), row_token, gend[-1:].astype(jnp.int32), tile_expert, tile_valid


def _row_copy(src_ref, src_row, dst_ref, dst_row, sem):
    return pltpu.make_async_copy(src_ref.at[pl.ds(src_row, 1)], dst_ref.at[pl.ds(dst_row, 1)], sem)


GATHER_UNROLL = 8


def _prefetched_gather(issue, n_steps):
    i = pl.program_id(0)
    slot = i % 2

    @pl.when(i == 0)
    def _():
        issue(i, slot)

    @pl.when(i + 1 < n_steps)
    def _():
        issue(i + 1, 1 - slot)

    return slot


def _dispatch_kernel(tok_ref, total_ref, src_ref, o_ref, buf_ref, sems, *, rows, n_steps):
    def issue(step, slot):
        base = step * rows

        @pl.when(base < total_ref[0])
        def _():
            def body(r, c):
                _row_copy(src_ref, tok_ref[base + r], buf_ref.at[slot], r, sems.at[slot]).start()
                return c

            lax.fori_loop(0, rows, body, 0, unroll=min(GATHER_UNROLL, rows))

    slot = _prefetched_gather(issue, n_steps)
    used = pl.program_id(0) * rows < total_ref[0]

    @pl.when(used)
    def _():
        def drain(r, c):
            _row_copy(src_ref, 0, buf_ref.at[slot], r, sems.at[slot]).wait()
            return c

        lax.fori_loop(0, rows, drain, 0, unroll=min(GATHER_UNROLL, rows))
        o_ref[...] = buf_ref[slot].astype(o_ref.dtype)

    @pl.when(jnp.logical_not(used))
    def _():
        o_ref[...] = jnp.zeros_like(o_ref)


def _dispatch(h32, row_token, total_rows, rows):
    t, d = h32.shape
    n_rows = row_token.shape[0]
    n_steps = n_rows // rows
    blk = 2 * _nbytes((rows, d), BF16) + 3 * _nbytes((rows, d), F32)
    grid_spec = pltpu.PrefetchScalarGridSpec(
        num_scalar_prefetch=2,
        grid=(n_steps,),
        in_specs=[pl.BlockSpec(memory_space=pl.ANY)],
        out_specs=pl.BlockSpec((rows, d), lambda i, tok, total: (i, 0)),
        scratch_shapes=[pltpu.VMEM((2, rows, d), F32), pltpu.SemaphoreType.DMA((2,))],
    )
    return pl.pallas_call(
        functools.partial(_dispatch_kernel, rows=rows, n_steps=n_steps),
        grid_spec=grid_spec,
        out_shape=jax.ShapeDtypeStruct((n_rows, d), BF16),
        compiler_params=_params(("arbitrary",), blk),
        name="expert_dispatch",
    )(row_token, total_rows, h32)


def _combine_ln_kernel(*refs, alpha, n_mod, tm, n_steps):
    pos_ref, x_ref, rows_ref, wgt_ref, gate_ref, g_ref, b_ref = refs[:7]
    mod_refs = refs[7:7 + 2 * n_mod]
    xo_ref = refs[7 + 2 * n_mod]
    h_refs = refs[8 + 2 * n_mod:8 + 3 * n_mod]
    buf_ref, sems = refs[8 + 3 * n_mod:]

    def issue(step, slot):
        base = step * tm

        def body(r, c):
            for k in range(TOP_K):
                _row_copy(rows_ref, pos_ref[(base + r) * TOP_K + k], buf_ref.at[slot, k], r,
                          sems.at[slot]).start()
            return c

        lax.fori_loop(0, tm, body, 0, unroll=min(GATHER_UNROLL, tm))

    slot = _prefetched_gather(issue, n_steps)

    def drain(r, c):
        for k in range(TOP_K):
            _row_copy(rows_ref, 0, buf_ref.at[slot, k], r, sems.at[slot]).wait()
        return c

    lax.fori_loop(0, tm, drain, 0, unroll=min(GATHER_UNROLL, tm))
    wgt = wgt_ref[...]
    y = wgt[:, 0:1] * buf_ref[slot, 0]
    for k in range(1, TOP_K):
        y = y + wgt[:, k:k + 1] * buf_ref[slot, k]
    v = alpha * x_ref[...] + gate_ref[0] * y
    xn = _ln_rows(v, g_ref[...], b_ref[...])
    xo_ref[...] = xn
    _emit_mods(xn, mod_refs, h_refs)


def _combine_ln(x2, expert_rows, pos, top_w, gate1, ln_g, ln_b, mods, mod_dtypes, seq, alpha):
    t, d = x2.shape
    n_mod = len(mods)
    tm = _pick(seq, (128, 64, 32, 16, 8))
    tpb = seq // tm
    n_steps = t // tm
    tile = pl.BlockSpec((tm, d), lambda i, pos: (i, 0))
    vec = pl.BlockSpec((1, 1, d), lambda i, pos: (i // tpb, 0, 0))
    par = pl.BlockSpec((1, d), lambda i, pos: (0, 0))
    wgt = pl.BlockSpec((tm, TOP_K), lambda i, pos: (i, 0))
    in_specs = [tile, pl.BlockSpec(memory_space=pl.ANY), wgt, vec, par, par] + [vec, vec] * n_mod
    args = [x2, expert_rows, top_w, gate1, ln_g.reshape(1, d), ln_b.reshape(1, d)]
    for sh, sc1 in mods:
        args += [sh, sc1]
    out_shape = [jax.ShapeDtypeStruct((t, d), F32)] + \
        [jax.ShapeDtypeStruct((t, d), dt) for dt in mod_dtypes]
    blk = 2 * (2 * _nbytes((tm, d), F32) + sum(_nbytes((tm, d), dt) for dt in mod_dtypes)) \
        + (2 * TOP_K + 4) * _nbytes((tm, d), F32)
    grid_spec = pltpu.PrefetchScalarGridSpec(
        num_scalar_prefetch=1,
        grid=(n_steps,),
        in_specs=in_specs,
        out_specs=[tile] * (1 + n_mod),
        scratch_shapes=[pltpu.VMEM((2, TOP_K, tm, d), F32), pltpu.SemaphoreType.DMA((2,))],
    )
    outs = pl.pallas_call(
        functools.partial(_combine_ln_kernel, alpha=alpha, n_mod=n_mod, tm=tm, n_steps=n_steps),
        grid_spec=grid_spec,
        out_shape=out_shape,
        compiler_params=_params(("arbitrary",), blk),
        name="combine_ln",
    )(pos, *args)
    return outs[0], list(outs[1:])


def kernel(x, c, ada_w, ada_b, ada_table, kv_table, a_w_qkv, a_w_o, kv_w, kv_b_f, b_w_q, b_w_o,
           ln_g, ln_b, ffn_w_up, ffn_w_down, moe_w_router, moe_b_router, moe_w_up, moe_w_down):
    bsz, seq, d = x.shape
    t = bsz * seq
    depth, n_mod = ada_table.shape[0], ada_table.shape[1]
    n_a = a_w_qkv.shape[0]
    n_heads = kv_w.shape[1] - 2 * d
    head_dim = d // n_heads
    n_experts = moe_w_router.shape[-1]
    alpha = (2.0 * depth) ** 0.25
    expert_tile = _pick(TOP_K * t, (512, 256, 128, 64, 32, 16))
    q_scale = head_dim ** -0.5 * LOG2E
    stick_heads = STICK_HEADS_PER_STEP if n_heads % STICK_HEADS_PER_STEP == 0 else 1
    fox_heads = FOX_HEADS_PER_STEP if n_heads % FOX_HEADS_PER_STEP == 0 else 1

    mod = _ada_mod(c, ada_w, ada_b).reshape(bsz, n_mod, d)

    def vec3(v):
        return v.reshape(bsz, 1, d)

    def layer_mods(l):
        m = mod + ada_table[l][None]
        return [m[:, i] for i in range(n_mod)]

    def mix_mod(l):
        m = layer_mods(l)
        return vec3(m[0]), vec3(1.0 + m[1])

    kv_mod = (vec3(mod[:, 0] + kv_table[0]), vec3(1.0 + mod[:, 1] + kv_table[1]))

    x2 = x.reshape(t, d)
    h = _modcast(x2, *mix_mod(0), seq, BF16)
    h_kv = None
    shared = None
    for l in range(depth):
        m = layer_mods(l)
        gate_mix1, gate_ffn1 = vec3(1.0 + m[2]), vec3(1.0 + m[5])
        ffn_mod = (vec3(m[3]), vec3(1.0 + m[4]))
        is_moe = l % 2 == 1

        if l < n_a:
            qkv = _matmul(h, a_w_qkv, l, 0, 3 * d, BF16, scaled_cols=d, col_scale=-q_scale)
            o = _stick_attention(qkv.reshape(bsz, seq, 3 * d), n_heads, head_dim, BF16,
                                 stick_heads)
            mix = _matmul(o.reshape(t, d), a_w_o, l, 0, d, BRANCH_DTYPE)
        else:
            if shared is None:
                kv = _matmul(h_kv, kv_w, None, 0, 2 * d, BF16)
                log_f = _forget_cumsum(h_kv.reshape(bsz, seq, d), kv_w[:, 2 * d:], kv_b_f)
                shared = (kv.reshape(bsz, seq, 2 * d), jnp.transpose(log_f, (0, 2, 1)))
            j = l - n_a
            q = _matmul(h, b_w_q, j, 0, d, BF16, scaled_cols=d, col_scale=q_scale)
            o = _fox_attention(q.reshape(bsz, seq, d), shared[0], shared[1],
                               n_heads, head_dim, BF16, fox_heads)
            mix = _matmul(o.reshape(t, d), b_w_o, j, 0, d, BRANCH_DTYPE)
        x2, (h_ffn,) = _residual_ln(x2, mix, gate_mix1, ln_g[l, 0], ln_b[l, 0], [ffn_mod],
                                    [F32 if is_moe else BF16], seq, alpha)

        next_mods, next_dtypes = [], []
        if l + 1 < depth:
            next_mods.append(mix_mod(l + 1))
            next_dtypes.append(BF16)
            if l + 1 == n_a:
                next_mods.append(kv_mod)
                next_dtypes.append(BF16)
        if not is_moe:
            act = _swiglu_up(h_ffn, ffn_w_up, l // 2)
            y = _matmul(act, ffn_w_down, l // 2, 0, d, BRANCH_DTYPE)
            x2, hs = _residual_ln(x2, y, gate_ffn1, ln_g[l, 1], ln_b[l, 1], next_mods,
                                  next_dtypes, seq, alpha)
        else:
            e = l // 2
            top_idx, top_w = _router(h_ffn, moe_w_router[e], moe_b_router[e])
            pos, row_token, total_rows, tile_expert, tile_valid = _route_plan(
                top_idx, n_experts, expert_tile)
            hg = _dispatch(h_ffn, row_token, total_rows, min(expert_tile, 256))
            act = _grouped_swiglu_up(hg, moe_w_up, e, tile_expert, tile_valid, expert_tile)
            rows = _grouped_matmul(act, moe_w_down, e, tile_expert, tile_valid, expert_tile, F32)
            x2, hs = _combine_ln(x2, rows, pos, top_w, gate_ffn1, ln_g[l, 1], ln_b[l, 1],
                                 next_mods, next_dtypes, seq, alpha)
        if hs:
            h = hs[0]
            if len(hs) > 1:
                h_kv = hs[1]
    return x2.reshape(bsz, seq, d)
```

```python
import functools

import jax
import jax.numpy as jnp
from jax import lax
from jax.experimental import pallas as pl
from jax.experimental.pallas import tpu as pltpu

F32 = jnp.float32
BF16 = jnp.bfloat16

LN_EPS = 1e-5
TOP_K = 2
NEG_INF = -1e30
LOG2E = 1.4426950408889634
LANE = 128
FORGET_LANES = LANE
ROUTER_LANES = LANE
BRANCH_DTYPE = BF16
STICK_HEADS_PER_STEP = 8
FOX_HEADS_PER_STEP = 4

V7X_VMEM_BYTES = 64 * 1024 * 1024
VMEM_HEADROOM_BYTES = 12 * 1024 * 1024
VMEM_CAP_BYTES = V7X_VMEM_BYTES - 6 * 1024 * 1024


def _params(semantics, block_bytes):
    limit = min(int(block_bytes) + VMEM_HEADROOM_BYTES, VMEM_CAP_BYTES)
    return pltpu.CompilerParams(dimension_semantics=semantics, vmem_limit_bytes=limit)


def _nbytes(shape, dtype):
    n = 1
    for s in shape:
        n *= s
    return n * jnp.dtype(dtype).itemsize


def _pick(n, prefs):
    for p in prefs:
        if n % p == 0:
            return p
    return n


def _ada_kernel(c_ref, w_ref, b_ref, o_ref):
    c = c_ref[...]
    s = c * jax.nn.sigmoid(c)
    o_ref[...] = jnp.dot(s.astype(BF16), w_ref[...].astype(BF16),
                         preferred_element_type=F32) + b_ref[...]


def _ada_mod(c, ada_w, ada_b):
    bsz, d = c.shape
    n = ada_w.shape[1]
    rows = 8
    c8 = jnp.zeros((rows, d), F32).at[:bsz].set(c)
    tn = _pick(n, (512, 256, 128))
    blk = 2 * (_nbytes((rows, d), F32) + _nbytes((d, tn), F32) + 2 * _nbytes((rows, tn), F32))
    out = pl.pallas_call(
        _ada_kernel,
        grid=(n // tn,),
        in_specs=[pl.BlockSpec((rows, d), lambda j: (0, 0)),
                  pl.BlockSpec((d, tn), lambda j: (0, j)),
                  pl.BlockSpec((1, tn), lambda j: (0, j))],
        out_specs=pl.BlockSpec((rows, tn), lambda j: (0, j)),
        out_shape=jax.ShapeDtypeStruct((rows, n), F32),
        compiler_params=_params(("arbitrary",), blk),
        name="ada_mod",
    )(c8, ada_w, ada_b.reshape(1, n))
    return out[:bsz]


def _modcast_kernel(x_ref, sh_ref, sc_ref, o_ref):
    o_ref[...] = (x_ref[...] * sc_ref[0] + sh_ref[0]).astype(o_ref.dtype)


def _modcast(x2, shift, scale1, seq, out_dtype):
    t, d = x2.shape
    tm = _pick(seq, (256, 128, 64, 32, 16, 8))
    tpb = seq // tm
    vec = pl.BlockSpec((1, 1, d), lambda i: (i // tpb, 0, 0))
    blk = 2 * (_nbytes((tm, d), F32) + _nbytes((tm, d), out_dtype))
    return pl.pallas_call(
        _modcast_kernel,
        grid=(t // tm,),
        in_specs=[pl.BlockSpec((tm, d), lambda i: (i, 0)), vec, vec],
        out_specs=pl.BlockSpec((tm, d), lambda i: (i, 0)),
        out_shape=jax.ShapeDtypeStruct((t, d), out_dtype),
        compiler_params=_params(("arbitrary",), blk),
        name="modcast",
    )(x2, shift, scale1)


def _weight_columns(w_hbm, w_index, col, width):
    src = w_hbm if w_index is None else w_hbm.at[w_index]
    return src.at[:, pl.ds(pl.multiple_of(col, width), width)]


def _stage_weights(w_hbm, w_index, cols, stage_ref, wb_ref, sems, width):
    j = pl.program_id(0)
    n_steps = pl.num_programs(0)

    def copies(jj):
        return [pltpu.make_async_copy(_weight_columns(w_hbm, w_index, c, width),
                                      stage_ref.at[s], sems.at[s])
                for s, c in enumerate(cols(jj))]

    @pl.when(pl.program_id(1) == 0)
    def _():
        @pl.when(j == 0)
        def _():
            for cp in copies(j):
                cp.start()

        for s, cp in enumerate(copies(j)):
            cp.wait()
            wb_ref[s] = stage_ref[s].astype(BF16)

        @pl.when(j + 1 < n_steps)
        def _():
            for cp in copies(j + 1):
                cp.start()


def _mm_kernel(a_ref, w_hbm, o_ref, stage_ref, wb_ref, sems,
               *, w_index, col0, tn, scaled_blocks, col_scale):
    _stage_weights(w_hbm, w_index, lambda jj: [col0 + jj * tn], stage_ref, wb_ref, sems, tn)
    res = jnp.dot(a_ref[...], wb_ref[0], preferred_element_type=F32)
    if scaled_blocks:
        res = res * jnp.where(pl.program_id(0) < scaled_blocks, col_scale, 1.0)
    o_ref[...] = res.astype(o_ref.dtype)


MM_BLOCK_ELEMS = 4 * 1024 * 1024


def _mm_tiles(m, k, n):
    tn = _pick(n, tuple(c for c in (1024, 512, 256, 128) if c * k <= MM_BLOCK_ELEMS))
    tm = _pick(m, tuple(t for t in (1024, 512, 256, 128, 64, 32, 16, 8) if t * k <= MM_BLOCK_ELEMS))
    return tm, tn


def _matmul(a, w, w_index, col0, n, out_dtype, scaled_cols=0, col_scale=1.0):
    m, k = a.shape
    tm, tn = _mm_tiles(m, k, n)
    assert scaled_cols % tn == 0 and col0 % tn == 0
    blk = (2 * (_nbytes((tm, k), BF16) + _nbytes((tm, tn), out_dtype))
           + _nbytes((k, tn), F32) + _nbytes((k, tn), BF16) + _nbytes((tm, tn), F32))
    kern = functools.partial(_mm_kernel, w_index=w_index, col0=col0, tn=tn,
                             scaled_blocks=scaled_cols // tn, col_scale=col_scale)
    return pl.pallas_call(
        kern,
        grid=(n // tn, m // tm),
        in_specs=[pl.BlockSpec((tm, k), lambda j, i: (i, 0)),
                  pl.BlockSpec(memory_space=pl.ANY)],
        out_specs=pl.BlockSpec((tm, tn), lambda j, i: (i, j)),
        out_shape=jax.ShapeDtypeStruct((m, n), out_dtype),
        scratch_shapes=[pltpu.VMEM((1, k, tn), F32), pltpu.VMEM((1, k, tn), BF16),
                        pltpu.SemaphoreType.DMA((1,))],
        compiler_params=_params(("arbitrary", "arbitrary"), blk),
        name="matmul",
    )(a, w)


def _swiglu_kernel(a_ref, w_hbm, o_ref, stage_ref, wb_ref, sems, *, w_index, tn, f):
    _stage_weights(w_hbm, w_index, lambda jj: [jj * tn, f + jj * tn], stage_ref, wb_ref, sems, tn)
    a = a_ref[...]
    g = jnp.dot(a, wb_ref[0], preferred_element_type=F32)
    u = jnp.dot(a, wb_ref[1], preferred_element_type=F32)
    o_ref[...] = (g * jax.nn.sigmoid(g) * u).astype(o_ref.dtype)


def _swiglu_up(a, w_up, w_index):
    m, k = a.shape
    f = w_up.shape[-1] // 2
    tn = _pick(f, tuple(c for c in (512, 256, 128) if 2 * c * k <= MM_BLOCK_ELEMS))
    tm = _pick(m, tuple(t for t in (1024, 512, 256, 128, 64, 32, 16, 8) if t * k <= MM_BLOCK_ELEMS))
    blk = (2 * (_nbytes((tm, k), BF16) + _nbytes((tm, tn), BF16))
           + 2 * (_nbytes((k, tn), F32) + _nbytes((k, tn), BF16)) + 3 * _nbytes((tm, tn), F32))
    return pl.pallas_call(
        functools.partial(_swiglu_kernel, w_index=w_index, tn=tn, f=f),
        grid=(f // tn, m // tm),
        in_specs=[pl.BlockSpec((tm, k), lambda j, i: (i, 0)),
                  pl.BlockSpec(memory_space=pl.ANY)],
        out_specs=pl.BlockSpec((tm, tn), lambda j, i: (i, j)),
        out_shape=jax.ShapeDtypeStruct((m, f), BF16),
        scratch_shapes=[pltpu.VMEM((2, k, tn), F32), pltpu.VMEM((2, k, tn), BF16),
                        pltpu.SemaphoreType.DMA((2,))],
        compiler_params=_params(("arbitrary", "arbitrary"), blk),
        name="swiglu_up",
    )(a, w_up)


def _expert_changed(te_ref, i):
    prev = te_ref[jnp.maximum(i - 1, 0)]
    return jnp.logical_or(i == 0, te_ref[i] != prev)


def _next_group_tile(tile_expert):
    n = tile_expert.shape[0]
    idx = jnp.arange(n, dtype=jnp.int32)
    first = jnp.concatenate([jnp.ones((1,), bool), tile_expert[1:] != tile_expert[:-1]])
    cand = jnp.where(first, idx, n)
    from_here = lax.cummin(cand, axis=0, reverse=True)
    return jnp.concatenate([from_here[1:], jnp.full((1,), n, jnp.int32)])


def _stage_expert_weights(w_hbm, w_index, te_ref, nxt_ref, cols, stage_ref, wb_ref, sems, width):
    j, i = pl.program_id(0), pl.program_id(1)
    n_j, n_i = pl.num_programs(0), pl.num_programs(1)

    def copies(expert, jj):
        src = w_hbm.at[w_index, expert]
        return [pltpu.make_async_copy(src.at[:, pl.ds(pl.multiple_of(c, width), width)],
                                      stage_ref.at[s], sems.at[s])
                for s, c in enumerate(cols(jj))]

    @pl.when(_expert_changed(te_ref, i))
    def _():
        @pl.when(jnp.logical_and(j == 0, i == 0))
        def _():
            for cp in copies(te_ref[0], j):
                cp.start()

        for s, cp in enumerate(copies(te_ref[i], j)):
            cp.wait()
            wb_ref[s] = stage_ref[s].astype(BF16)

        nxt = nxt_ref[i]

        @pl.when(nxt < n_i)
        def _():
            for cp in copies(te_ref[nxt], j):
                cp.start()

        @pl.when(jnp.logical_and(nxt >= n_i, j + 1 < n_j))
        def _():
            for cp in copies(te_ref[0], j + 1):
                cp.start()


def _gswiglu_kernel(te_ref, tv_ref, nxt_ref, a_ref, w_hbm, o_ref, stage_ref, wb_ref, sems,
                    *, w_index, tn, f):
    i = pl.program_id(1)
    _stage_expert_weights(w_hbm, w_index, te_ref, nxt_ref, lambda jj: [jj * tn, f + jj * tn],
                          stage_ref, wb_ref, sems, tn)

    @pl.when(tv_ref[i] > 0)
    def _():
        a = a_ref[...]
        g = jnp.dot(a, wb_ref[0], preferred_element_type=F32)
        u = jnp.dot(a, wb_ref[1], preferred_element_type=F32)
        o_ref[...] = (g * jax.nn.sigmoid(g) * u).astype(o_ref.dtype)

    @pl.when(tv_ref[i] == 0)
    def _():
        o_ref[...] = jnp.zeros_like(o_ref)


def _grouped_swiglu_up(a, w_up, w_index, tile_expert, tile_valid, next_group, tile):
    p, k = a.shape
    f = w_up.shape[-1] // 2
    tn = _pick(f, tuple(c for c in (512, 256, 128) if 2 * c * k <= MM_BLOCK_ELEMS))
    blk = (2 * (_nbytes((tile, k), BF16) + _nbytes((tile, tn), BF16))
           + 2 * (_nbytes((k, tn), F32) + _nbytes((k, tn), BF16)) + 3 * _nbytes((tile, tn), F32))
    grid_spec = pltpu.PrefetchScalarGridSpec(
        num_scalar_prefetch=3,
        grid=(f // tn, p // tile),
        in_specs=[pl.BlockSpec((tile, k), lambda j, i, te, tv, nx: (i, 0)),
                  pl.BlockSpec(memory_space=pl.ANY)],
        out_specs=pl.BlockSpec((tile, tn), lambda j, i, te, tv, nx: (i, j)),
        scratch_shapes=[pltpu.VMEM((2, k, tn), F32), pltpu.VMEM((2, k, tn), BF16),
                        pltpu.SemaphoreType.DMA((2,))],
    )
    return pl.pallas_call(
        functools.partial(_gswiglu_kernel, w_index=w_index, tn=tn, f=f),
        grid_spec=grid_spec,
        out_shape=jax.ShapeDtypeStruct((p, f), BF16),
        compiler_params=_params(("arbitrary", "arbitrary"), blk),
        name="expert_swiglu_up",
    )(tile_expert, tile_valid, next_group, a, w_up)


def _gmm_kernel(te_ref, tv_ref, nxt_ref, a_ref, w_hbm, o_ref, stage_ref, wb_ref, sems,
                *, w_index, tn):
    i = pl.program_id(1)
    _stage_expert_weights(w_hbm, w_index, te_ref, nxt_ref, lambda jj: [jj * tn],
                          stage_ref, wb_ref, sems, tn)

    @pl.when(tv_ref[i] > 0)
    def _():
        o_ref[...] = jnp.dot(a_ref[...], wb_ref[0],
                             preferred_element_type=F32).astype(o_ref.dtype)

    @pl.when(tv_ref[i] == 0)
    def _():
        o_ref[...] = jnp.zeros_like(o_ref)


def _grouped_matmul(a, w, w_index, tile_expert, tile_valid, next_group, tile, out_dtype):
    p, k = a.shape
    n = w.shape[-1]
    tn = _pick(n, tuple(c for c in (2048, 1024, 512, 256, 128) if k * c <= MM_BLOCK_ELEMS))
    blk = (2 * (_nbytes((tile, k), BF16) + _nbytes((tile, tn), out_dtype))
           + _nbytes((k, tn), F32) + _nbytes((k, tn), BF16) + _nbytes((tile, tn), F32))
    grid_spec = pltpu.PrefetchScalarGridSpec(
        num_scalar_prefetch=3,
        grid=(n // tn, p // tile),
        in_specs=[pl.BlockSpec((tile, k), lambda j, i, te, tv, nx: (i, 0)),
                  pl.BlockSpec(memory_space=pl.ANY)],
        out_specs=pl.BlockSpec((tile, tn), lambda j, i, te, tv, nx: (i, j)),
        scratch_shapes=[pltpu.VMEM((1, k, tn), F32), pltpu.VMEM((1, k, tn), BF16),
                        pltpu.SemaphoreType.DMA((1,))],
    )
    return pl.pallas_call(
        functools.partial(_gmm_kernel, w_index=w_index, tn=tn),
        grid_spec=grid_spec,
        out_shape=jax.ShapeDtypeStruct((p, n), out_dtype),
        compiler_params=_params(("arbitrary", "arbitrary"), blk),
        name="expert_matmul",
    )(tile_expert, tile_valid, next_group, a, w)


def _softplus(z):
    return jnp.maximum(z, 0.0) + jnp.log(1.0 + jnp.exp(-jnp.abs(z)))


def _neg_abs(x):
    bits = lax.bitcast_convert_type(x, jnp.uint32) | jnp.uint32(0x80000000)
    return lax.bitcast_convert_type(bits, F32)


def _lanes(x, n):
    reps = n // x.shape[1]
    return x if reps == 1 else jnp.concatenate([x] * reps, axis=1)


def _stick_kernel(q_ref, k_ref, v_ref, o_ref, acc_ref, carry_ref, *, tq, tk, heads, hd):
    qi = pl.program_id(2)
    n_sub = tq // tk
    row = lax.broadcasted_iota(jnp.int32, (tk, tk), 0)
    col = lax.broadcasted_iota(jnp.int32, (tk, tk), 1)
    later = (row > col).astype(BF16)
    acc_ref[...] = jnp.zeros_like(acc_ref)
    carry_ref[...] = jnp.zeros_like(carry_ref)

    def block(g, r0, k0, diag_off):
        nr = tq - r0
        lanes = slice(g * hd, (g + 1) * hd)
        q = q_ref[0, r0:tq, lanes]
        k = k_ref[0, pl.ds(k0, tk), lanes]
        v = v_ref[0, pl.ds(k0, tk), lanes]
        s = lax.dot_general(q, k, (((1,), (1,)), ((), ())), preferred_element_type=F32)
        soft = jnp.log(1.0 + jnp.exp2(_neg_abs(s))) * LOG2E
        log_keep = jnp.minimum(s, 0.0) - soft
        if diag_off is not None:
            q_pos = r0 + lax.broadcasted_iota(jnp.int32, (nr, tk), 0)
            k_pos = diag_off + lax.broadcasted_iota(jnp.int32, (nr, tk), 1)
            mask = k_pos < q_pos
            log_keep = jnp.where(mask, log_keep, 0.0)
        suffix = jnp.dot(log_keep.astype(BF16), later, preferred_element_type=F32)
        carry = carry_ref[g, r0:tq, :]
        w = jnp.exp2((log_keep - s) + suffix + _lanes(carry, tk))
        if diag_off is not None:
            w = jnp.where(mask, w, 0.0)
        acc_ref[g, r0:tq, :] += jnp.dot(w.astype(BF16), v, preferred_element_type=F32)
        carry_ref[g, r0:tq, :] = carry + jnp.sum(log_keep, axis=1, keepdims=True)

    diag0 = qi * tq
    for sub in range(n_sub - 1, -1, -1):
        for g in range(heads):
            block(g, sub * tk, pl.multiple_of(diag0 + sub * tk, tk), sub * tk)

    def body(j, c):
        k0 = pl.multiple_of(diag0 - (j + 1) * tk, tk)
        for g in range(heads):
            block(g, 0, k0, None)
        return c

    lax.fori_loop(0, qi * n_sub, body, 0)
    for g in range(heads):
        o_ref[0, :, g * hd:(g + 1) * hd] = acc_ref[g].astype(o_ref.dtype)


def _attn_tiles(seq):
    tq = _pick(seq, (512, 256, 128))
    tk = min(tq, 256)
    return tq, tk


def _stick_attention(qkv, n_heads, head_dim, out_dtype, heads_per_step):
    bsz, seq, _ = qkv.shape
    d = n_heads * head_dim
    g = heads_per_step
    tq, tk = _attn_tiles(seq)
    width = g * head_dim
    kern = functools.partial(_stick_kernel, tq=tq, tk=tk, heads=g, hd=head_dim)
    blk = (2 * (2 * _nbytes((tq, width), BF16) + 2 * _nbytes((seq, width), BF16))
           + g * (_nbytes((tq, head_dim), F32) + _nbytes((tq, LANE), F32))
           + 8 * g * _nbytes((tq, tk), F32))
    groups = n_heads // g
    return pl.pallas_call(
        kern,
        grid=(bsz, groups, seq // tq),
        in_specs=[pl.BlockSpec((1, tq, width), lambda b, h, i: (b, i, h)),
                  pl.BlockSpec((1, seq, width), lambda b, h, i: (b, 0, groups + h)),
                  pl.BlockSpec((1, seq, width), lambda b, h, i: (b, 0, 2 * groups + h))],
        out_specs=pl.BlockSpec((1, tq, width), lambda b, h, i: (b, i, h)),
        out_shape=jax.ShapeDtypeStruct((bsz, seq, d), out_dtype),
        scratch_shapes=[pltpu.VMEM((g, tq, head_dim), F32), pltpu.VMEM((g, tq, LANE), F32)],
        compiler_params=_params(("arbitrary", "arbitrary", "arbitrary"), blk),
        name="stick_attention",
    )(qkv, qkv, qkv)


def _fox_kernel(q_ref, k_ref, v_ref, fk_ref, o_ref, m_ref, fqb_ref, acc_ref,
                *, tq, tk, tr, heads, hd):
    qi = pl.program_id(2)
    assert tq == tk
    m_ref[...] = jnp.full_like(m_ref, NEG_INF)
    acc_ref[...] = jnp.zeros_like(acc_ref)
    for g in range(heads):
        f_row = fk_ref[0, g, pl.ds(qi, 1), :]
        for c in range(tk // LANE):
            chunk = f_row[:, c * LANE:(c + 1) * LANE]
            fqb_ref[g, c * LANE:(c + 1) * LANE, :] = jnp.broadcast_to(chunk, (LANE, LANE)).T

    def block(g, r0, r1, kb, nk, masked):
        nr = r1 - r0
        lanes = slice(g * hd, (g + 1) * hd)
        k0 = pl.multiple_of(kb * tk, tk)
        q = q_ref[0, r0:r1, lanes]
        k = k_ref[0, pl.ds(k0, nk), lanes]
        v1 = jnp.concatenate([v_ref[0, pl.ds(k0, nk), lanes], jnp.ones((nk, LANE), BF16)], axis=1)
        fk = fk_ref[0, g, pl.ds(kb, 1), :][:, :nk]
        u = lax.dot_general(q, k, (((1,), (1,)), ((), ())), preferred_element_type=F32) - fk
        if masked:
            q_pos = r0 + lax.broadcasted_iota(jnp.int32, (nr, nk), 0)
            k_pos = lax.broadcasted_iota(jnp.int32, (nr, nk), 1)
            u = jnp.where(k_pos <= q_pos, u, NEG_INF)
        fq = fqb_ref[g, r0:r1, :]
        m_prev = m_ref[g, r0:r1, :]
        m_new = jnp.maximum(m_prev, jnp.max(u, axis=1, keepdims=True) + fq)
        alpha = jnp.exp2(m_prev - m_new)
        p = jnp.exp2(u + _lanes(fq - m_new, nk))
        pv = jnp.dot(p.astype(BF16), v1, preferred_element_type=F32)
        acc_ref[g, r0:r1, :] = _lanes(alpha, hd + LANE) * acc_ref[g, r0:r1, :] + pv
        m_ref[g, r0:r1, :] = m_new

    def body(j, c):
        for g in range(heads):
            block(g, 0, tq, j, tk, False)
        return c

    lax.fori_loop(0, qi, body, 0)
    for r0 in range(0, tq, tr):
        for g in range(heads):
            block(g, r0, r0 + tr, qi, r0 + tr, True)
    for g in range(heads):
        acc = acc_ref[g]
        o_ref[0, :, g * hd:(g + 1) * hd] = (acc[:, :hd] / acc[:, hd:]).astype(o_ref.dtype)


def _fox_attention(q, kv, log_f_cum, n_heads, head_dim, out_dtype, heads_per_step):
    bsz, seq, d = q.shape
    g = heads_per_step
    assert head_dim == LANE
    tq = tk = _pick(seq, (512, 256, 128))
    tr = min(tq, 256)
    width = g * head_dim
    groups = n_heads // g
    fk = log_f_cum.reshape(bsz, n_heads, seq // tk, tk)
    kern = functools.partial(_fox_kernel, tq=tq, tk=tk, tr=tr, heads=g, hd=head_dim)
    blk = (2 * (2 * _nbytes((tq, width), BF16) + 2 * _nbytes((seq, width), BF16)
                + g * _nbytes((seq // tk, tk), F32))
           + 4 * g * _nbytes((tq, LANE), F32) + 8 * g * _nbytes((tq, tk), F32))
    return pl.pallas_call(
        kern,
        grid=(bsz, groups, seq // tq),
        in_specs=[pl.BlockSpec((1, tq, width), lambda b, h, i: (b, i, h)),
                  pl.BlockSpec((1, seq, width), lambda b, h, i: (b, 0, h)),
                  pl.BlockSpec((1, seq, width), lambda b, h, i: (b, 0, groups + h)),
                  pl.BlockSpec((1, g, seq // tk, tk), lambda b, h, i: (b, h, 0, 0))],
        out_specs=pl.BlockSpec((1, tq, width), lambda b, h, i: (b, i, h)),
        out_shape=jax.ShapeDtypeStruct((bsz, seq, d), out_dtype),
        scratch_shapes=[pltpu.VMEM((g, tq, LANE), F32), pltpu.VMEM((g, tq, LANE), F32),
                        pltpu.VMEM((g, tq, head_dim + LANE), F32)],
        compiler_params=_params(("arbitrary", "arbitrary", "arbitrary"), blk),
        name="fox_attention",
    )(q, kv, kv, fk)


def _forget_kernel(h_ref, w_ref, b_ref, o_ref, carry_ref, *, ts):
    @pl.when(pl.program_id(1) == 0)
    def _():
        carry_ref[...] = jnp.zeros_like(carry_ref)

    logit = jnp.dot(h_ref[0], w_ref[...].astype(BF16), preferred_element_type=F32) + b_ref[...]
    log_f = -_softplus(-logit) * LOG2E
    row = lax.broadcasted_iota(jnp.int32, (ts, ts), 0)
    col = lax.broadcasted_iota(jnp.int32, (ts, ts), 1)
    upto = (col <= row).astype(F32)
    csum = jnp.dot(upto, log_f, preferred_element_type=F32,
                   precision=lax.Precision.HIGHEST) + carry_ref[...]
    o_ref[0] = csum
    carry_ref[...] = csum[ts - 1:ts, :]


def _forget_cumsum(h3, w_f, b_f):
    bsz, seq, d = h3.shape
    n_heads = w_f.shape[1]
    lanes = FORGET_LANES
    w_pad = jnp.zeros((d, lanes), F32).at[:, :n_heads].set(w_f)
    b_pad = jnp.zeros((1, lanes), F32).at[0, :n_heads].set(b_f.astype(F32))
    ts = _pick(seq, (256, 128, 64, 32, 16, 8))
    blk = 2 * (_nbytes((ts, d), BF16) + _nbytes((d, lanes), F32) + _nbytes((ts, lanes), F32)) \
        + 4 * _nbytes((ts, ts), F32)
    out = pl.pallas_call(
        functools.partial(_forget_kernel, ts=ts),
        grid=(bsz, seq // ts),
        in_specs=[pl.BlockSpec((1, ts, d), lambda b, i: (b, i, 0)),
                  pl.BlockSpec((d, lanes), lambda b, i: (0, 0)),
                  pl.BlockSpec((1, lanes), lambda b, i: (0, 0))],
        out_specs=pl.BlockSpec((1, ts, lanes), lambda b, i: (b, i, 0)),
        out_shape=jax.ShapeDtypeStruct((bsz, seq, lanes), F32),
        scratch_shapes=[pltpu.VMEM((1, lanes), F32)],
        compiler_params=_params(("arbitrary", "arbitrary"), blk),
        name="forget_cumsum",
    )(h3, w_pad, b_pad)
    return out[:, :, :n_heads]


def _ln_rows(v, g, b):
    mu = jnp.mean(v, axis=-1, keepdims=True)
    cen = v - mu
    var = jnp.mean(cen * cen, axis=-1, keepdims=True)
    return cen * lax.rsqrt(var + LN_EPS) * g + b


def _emit_mods(xn, mod_refs, out_refs):
    for m in range(len(out_refs)):
        sh_ref, sc_ref = mod_refs[2 * m], mod_refs[2 * m + 1]
        out_refs[m][...] = (xn * sc_ref[0] + sh_ref[0]).astype(out_refs[m].dtype)


def _split_bf16(v):
    hi = v.astype(BF16)
    return hi, (v - hi.astype(F32)).astype(BF16)


def _route_top2(h, w_ref, b_ref, idx_ref, wgt_ref, n_experts):
    h_hi, h_lo = _split_bf16(h)
    w_hi, w_lo = _split_bf16(w_ref[...])
    logits = (jnp.dot(h_hi, w_hi, preferred_element_type=F32)
              + jnp.dot(h_lo, w_hi, preferred_element_type=F32)
              + jnp.dot(h_hi, w_lo, preferred_element_type=F32)) + b_ref[...]
    lane = lax.broadcasted_iota(jnp.int32, logits.shape, 1)
    logits = jnp.where(lane < n_experts, logits, -jnp.inf)
    big = jnp.int32(ROUTER_LANES)
    v1 = jnp.max(logits, axis=1, keepdims=True)
    i1 = jnp.min(jnp.where(logits == v1, lane, big), axis=1, keepdims=True)
    rest = jnp.where(lane == i1, -jnp.inf, logits)
    v2 = jnp.max(rest, axis=1, keepdims=True)
    i2 = jnp.min(jnp.where(rest == v2, lane, big), axis=1, keepdims=True)
    e2 = jnp.exp(v2 - v1)
    w1 = 1.0 / (1.0 + e2)
    w2 = e2 / (1.0 + e2)
    idx_ref[...] = jnp.where(lane == 0, i1, jnp.where(lane == 1, i2, 0))
    wgt_ref[...] = jnp.where(lane == 0, w1, jnp.where(lane == 1, w2, 0.0))


def _ln_kernel(*refs, alpha, n_mod, n_experts):
    x_ref, y_ref, gate_ref, g_ref, b_ref = refs[:5]
    mod_refs = refs[5:5 + 2 * n_mod]
    n_in = 5 + 2 * n_mod + (2 if n_experts else 0)
    xo_ref = refs[n_in]
    h_refs = refs[n_in + 1:n_in + 1 + n_mod]
    v = alpha * x_ref[...] + gate_ref[0] * y_ref[...].astype(F32)
    xn = _ln_rows(v, g_ref[...], b_ref[...])
    xo_ref[...] = xn
    _emit_mods(xn, mod_refs, h_refs)
    if n_experts:
        wr_ref, br_ref = refs[n_in - 2:n_in]
        idx_ref, wgt_ref = refs[n_in + 1 + n_mod:]
        _route_top2(xn * mod_refs[1][0] + mod_refs[0][0], wr_ref, br_ref, idx_ref, wgt_ref,
                    n_experts)


def _residual_ln(x2, y2, gate1, ln_g, ln_b, mods, mod_dtypes, seq, alpha, router=None):
    t, d = x2.shape
    n_mod = len(mods)
    tm = _pick(seq, (256, 128, 64, 32, 16, 8))
    tpb = seq // tm
    tile = pl.BlockSpec((tm, d), lambda i: (i, 0))
    vec = pl.BlockSpec((1, 1, d), lambda i: (i // tpb, 0, 0))
    par = pl.BlockSpec((1, d), lambda i: (0, 0))
    in_specs = [tile, tile, vec, par, par] + [vec, vec] * n_mod
    args = [x2, y2, gate1, ln_g.reshape(1, d), ln_b.reshape(1, d)]
    for sh, sc1 in mods:
        args += [sh, sc1]
    out_specs = [tile] * (1 + n_mod)
    out_shape = [jax.ShapeDtypeStruct((t, d), F32)] + \
        [jax.ShapeDtypeStruct((t, d), dt) for dt in mod_dtypes]
    n_experts = 0
    if router is not None:
        w_router, b_router = router
        n_experts = w_router.shape[1]
        lanes = ROUTER_LANES
        args += [jnp.zeros((d, lanes), F32).at[:, :n_experts].set(w_router),
                 jnp.zeros((1, lanes), F32).at[0, :n_experts].set(b_router.astype(F32))]
        in_specs += [pl.BlockSpec((d, lanes), lambda i: (0, 0)),
                     pl.BlockSpec((1, lanes), lambda i: (0, 0))]
        out_specs += [pl.BlockSpec((tm, lanes), lambda i: (i, 0))] * 2
        out_shape += [jax.ShapeDtypeStruct((t, lanes), jnp.int32),
                      jax.ShapeDtypeStruct((t, lanes), F32)]
    blk = 2 * (_nbytes((tm, d), F32) + _nbytes((tm, d), y2.dtype) + _nbytes((tm, d), F32)
               + sum(_nbytes((tm, d), dt) for dt in mod_dtypes)) + 6 * _nbytes((tm, d), F32)
    outs = pl.pallas_call(
        functools.partial(_ln_kernel, alpha=alpha, n_mod=n_mod, n_experts=n_experts),
        grid=(t // tm,),
        in_specs=in_specs,
        out_specs=out_specs,
        out_shape=out_shape,
        compiler_params=_params(("arbitrary",), blk),
        name="residual_ln",
    )(*args)
    hs = list(outs[1:1 + n_mod])
    if router is not None:
        return outs[0], hs, outs[-2][:, :TOP_K], outs[-1][:, :TOP_K]
    return outs[0], hs


def _route_plan(top_idx, n_experts, tile):
    t = top_idx.shape[0]
    p = TOP_K * t
    e_flat = top_idx.reshape(p)
    onehot = (e_flat[:, None] == jnp.arange(n_experts, dtype=jnp.int32)[None, :]).astype(jnp.int32)
    csum = jnp.cumsum(onehot, axis=0)
    counts = csum[-1]
    rank = jnp.sum(csum * onehot, axis=1) - 1
    padded = ((counts + tile - 1) // tile) * tile
    gend = jnp.cumsum(padded)
    gstart = gend - padded
    pos = jnp.sum(onehot * gstart[None, :], axis=1) + rank
    n_rows = ((p + tile - 1) // tile + n_experts) * tile
    row_token = jnp.zeros((n_rows,), jnp.int32).at[pos].set(
        jnp.arange(p, dtype=jnp.int32) // TOP_K)
    tile_start = jnp.arange(n_rows // tile, dtype=jnp.int32) * tile
    tile_expert = jnp.sum((tile_start[:, None] >= gend[None, :]).astype(jnp.int32), axis=1)
    tile_valid = (tile_start < gend[-1]).astype(jnp.int32)
    tile_expert = jnp.minimum(tile_expert, n_experts - 1)
    return pos.astype(jnp.int32), row_token, gend[-1:].astype(jnp.int32), tile_expert, tile_valid


def _row_copy(src_ref, src_row, dst_ref, dst_row, sem):
    return pltpu.make_async_copy(src_ref.at[pl.ds(src_row, 1)], dst_ref.at[pl.ds(dst_row, 1)], sem)


GATHER_UNROLL = 8


def _prefetched_gather(issue, n_steps):
    i = pl.program_id(0)
    slot = i % 2

    @pl.when(i == 0)
    def _():
        issue(i, slot)

    @pl.when(i + 1 < n_steps)
    def _():
        issue(i + 1, 1 - slot)

    return slot


def _dispatch_kernel(tok_ref, total_ref, src_ref, o_ref, buf_ref, sems, *, rows, n_steps):
    def issue(step, slot):
        base = step * rows

        @pl.when(base < total_ref[0])
        def _():
            def body(r, c):
                _row_copy(src_ref, tok_ref[base + r], buf_ref.at[slot], r, sems.at[slot]).start()
                return c

            lax.fori_loop(0, rows, body, 0, unroll=min(GATHER_UNROLL, rows))

    slot = _prefetched_gather(issue, n_steps)
    used = pl.program_id(0) * rows < total_ref[0]

    @pl.when(used)
    def _():
        def drain(r, c):
            _row_copy(src_ref, 0, buf_ref.at[slot], r, sems.at[slot]).wait()
            return c

        lax.fori_loop(0, rows, drain, 0, unroll=min(GATHER_UNROLL, rows))
        o_ref[...] = buf_ref[slot].astype(o_ref.dtype)

    @pl.when(jnp.logical_not(used))
    def _():
        o_ref[...] = jnp.zeros_like(o_ref)


def _dispatch(h32, row_token, total_rows, rows):
    t, d = h32.shape
    n_rows = row_token.shape[0]
    n_steps = n_rows // rows
    blk = 2 * _nbytes((rows, d), BF16) + 3 * _nbytes((rows, d), F32)
    grid_spec = pltpu.PrefetchScalarGridSpec(
        num_scalar_prefetch=2,
        grid=(n_steps,),
        in_specs=[pl.BlockSpec(memory_space=pl.ANY)],
        out_specs=pl.BlockSpec((rows, d), lambda i, tok, total: (i, 0)),
        scratch_shapes=[pltpu.VMEM((2, rows, d), F32), pltpu.SemaphoreType.DMA((2,))],
    )
    return pl.pallas_call(
        functools.partial(_dispatch_kernel, rows=rows, n_steps=n_steps),
        grid_spec=grid_spec,
        out_shape=jax.ShapeDtypeStruct((n_rows, d), BF16),
        compiler_params=_params(("arbitrary",), blk),
        name="expert_dispatch",
    )(row_token, total_rows, h32)


def _combine_ln_kernel(*refs, alpha, n_mod, tm, n_steps):
    pos_ref, x_ref, rows_ref, wgt_ref, gate_ref, g_ref, b_ref = refs[:7]
    mod_refs = refs[7:7 + 2 * n_mod]
    xo_ref = refs[7 + 2 * n_mod]
    h_refs = refs[8 + 2 * n_mod:8 + 3 * n_mod]
    buf_ref, sems = refs[8 + 3 * n_mod:]

    def issue(step, slot):
        base = step * tm

        def body(r, c):
            for k in range(TOP_K):
                _row_copy(rows_ref, pos_ref[(base + r) * TOP_K + k], buf_ref.at[slot, k], r,
                          sems.at[slot]).start()
            return c

        lax.fori_loop(0, tm, body, 0, unroll=min(GATHER_UNROLL, tm))

    slot = _prefetched_gather(issue, n_steps)

    def drain(r, c):
        for k in range(TOP_K):
            _row_copy(rows_ref, 0, buf_ref.at[slot, k], r, sems.at[slot]).wait()
        return c

    lax.fori_loop(0, tm, drain, 0, unroll=min(GATHER_UNROLL, tm))
    wgt = wgt_ref[...]
    y = wgt[:, 0:1] * buf_ref[slot, 0]
    for k in range(1, TOP_K):
        y = y + wgt[:, k:k + 1] * buf_ref[slot, k]
    v = alpha * x_ref[...] + gate_ref[0] * y
    xn = _ln_rows(v, g_ref[...], b_ref[...])
    xo_ref[...] = xn
    _emit_mods(xn, mod_refs, h_refs)


def _combine_ln(x2, expert_rows, pos, top_w, gate1, ln_g, ln_b, mods, mod_dtypes, seq, alpha):
    t, d = x2.shape
    n_mod = len(mods)
    tm = _pick(seq, (128, 64, 32, 16, 8))
    tpb = seq // tm
    n_steps = t // tm
    tile = pl.BlockSpec((tm, d), lambda i, pos: (i, 0))
    vec = pl.BlockSpec((1, 1, d), lambda i, pos: (i // tpb, 0, 0))
    par = pl.BlockSpec((1, d), lambda i, pos: (0, 0))
    wgt = pl.BlockSpec((tm, TOP_K), lambda i, pos: (i, 0))
    in_specs = [tile, pl.BlockSpec(memory_space=pl.ANY), wgt, vec, par, par] + [vec, vec] * n_mod
    args = [x2, expert_rows, top_w, gate1, ln_g.reshape(1, d), ln_b.reshape(1, d)]
    for sh, sc1 in mods:
        args += [sh, sc1]
    out_shape = [jax.ShapeDtypeStruct((t, d), F32)] + \
        [jax.ShapeDtypeStruct((t, d), dt) for dt in mod_dtypes]
    blk = 2 * (2 * _nbytes((tm, d), F32) + sum(_nbytes((tm, d), dt) for dt in mod_dtypes)) \
        + (2 * TOP_K + 4) * _nbytes((tm, d), F32)
    grid_spec = pltpu.PrefetchScalarGridSpec(
        num_scalar_prefetch=1,
        grid=(n_steps,),
        in_specs=in_specs,
        out_specs=[tile] * (1 + n_mod),
        scratch_shapes=[pltpu.VMEM((2, TOP_K, tm, d), F32), pltpu.SemaphoreType.DMA((2,))],
    )
    outs = pl.pallas_call(
        functools.partial(_combine_ln_kernel, alpha=alpha, n_mod=n_mod, tm=tm, n_steps=n_steps),
        grid_spec=grid_spec,
        out_shape=out_shape,
        compiler_params=_params(("arbitrary",), blk),
        name="combine_ln",
    )(pos, *args)
    return outs[0], list(outs[1:])


def kernel(x, c, ada_w, ada_b, ada_table, kv_table, a_w_qkv, a_w_o, kv_w, kv_b_f, b_w_q, b_w_o,
           ln_g, ln_b, ffn_w_up, ffn_w_down, moe_w_router, moe_b_router, moe_w_up, moe_w_down):
    bsz, seq, d = x.shape
    t = bsz * seq
    depth, n_mod = ada_table.shape[0], ada_table.shape[1]
    n_a = a_w_qkv.shape[0]
    n_heads = kv_w.shape[1] - 2 * d
    head_dim = d // n_heads
    n_experts = moe_w_router.shape[-1]
    alpha = (2.0 * depth) ** 0.25
    expert_tile = _pick(TOP_K * t, (512, 256, 128, 64, 32, 16))
    q_scale = head_dim ** -0.5 * LOG2E
    stick_heads = STICK_HEADS_PER_STEP if n_heads % STICK_HEADS_PER_STEP == 0 else 1
    fox_heads = FOX_HEADS_PER_STEP if n_heads % FOX_HEADS_PER_STEP == 0 else 1

    mod = _ada_mod(c, ada_w, ada_b).reshape(bsz, n_mod, d)

    def vec3(v):
        return v.reshape(bsz, 1, d)

    def layer_mods(l):
        m = mod + ada_table[l][None]
        return [m[:, i] for i in range(n_mod)]

    def mix_mod(l):
        m = layer_mods(l)
        return vec3(m[0]), vec3(1.0 + m[1])

    kv_mod = (vec3(mod[:, 0] + kv_table[0]), vec3(1.0 + mod[:, 1] + kv_table[1]))

    x2 = x.reshape(t, d)
    h = _modcast(x2, *mix_mod(0), seq, BF16)
    h_kv = None
    shared = None
    for l in range(depth):
        m = layer_mods(l)
        gate_mix1, gate_ffn1 = vec3(1.0 + m[2]), vec3(1.0 + m[5])
        ffn_mod = (vec3(m[3]), vec3(1.0 + m[4]))
        is_moe = l % 2 == 1

        if l < n_a:
            qkv = _matmul(h, a_w_qkv, l, 0, 3 * d, BF16, scaled_cols=d, col_scale=-q_scale)
            o = _stick_attention(qkv.reshape(bsz, seq, 3 * d), n_heads, head_dim, BF16,
                                 stick_heads)
            mix = _matmul(o.reshape(t, d), a_w_o, l, 0, d, BRANCH_DTYPE)
        else:
            if shared is None:
                kv = _matmul(h_kv, kv_w, None, 0, 2 * d, BF16)
                log_f = _forget_cumsum(h_kv.reshape(bsz, seq, d), kv_w[:, 2 * d:], kv_b_f)
                shared = (kv.reshape(bsz, seq, 2 * d), jnp.transpose(log_f, (0, 2, 1)))
            j = l - n_a
            q = _matmul(h, b_w_q, j, 0, d, BF16, scaled_cols=d, col_scale=q_scale)
            o = _fox_attention(q.reshape(bsz, seq, d), shared[0], shared[1],
                               n_heads, head_dim, BF16, fox_heads)
            mix = _matmul(o.reshape(t, d), b_w_o, j, 0, d, BRANCH_DTYPE)
        if is_moe:
            x2, (h_ffn,), top_idx, top_w = _residual_ln(
                x2, mix, gate_mix1, ln_g[l, 0], ln_b[l, 0], [ffn_mod], [F32], seq, alpha,
                router=(moe_w_router[l // 2], moe_b_router[l // 2]))
        else:
            x2, (h_ffn,) = _residual_ln(x2, mix, gate_mix1, ln_g[l, 0], ln_b[l, 0], [ffn_mod],
                                        [BF16], seq, alpha)

        next_mods, next_dtypes = [], []
        if l + 1 < depth:
            next_mods.append(mix_mod(l + 1))
            next_dtypes.append(BF16)
            if l + 1 == n_a:
                next_mods.append(kv_mod)
                next_dtypes.append(BF16)
        if not is_moe:
            act = _swiglu_up(h_ffn, ffn_w_up, l // 2)
            y = _matmul(act, ffn_w_down, l // 2, 0, d, BRANCH_DTYPE)
            x2, hs = _residual_ln(x2, y, gate_ffn1, ln_g[l, 1], ln_b[l, 1], next_mods,
                                  next_dtypes, seq, alpha)
        else:
            e = l // 2
            pos, row_token, total_rows, tile_expert, tile_valid = _route_plan(
                top_idx, n_experts, expert_tile)
            hg = _dispatch(h_ffn, row_token, total_rows, min(expert_tile, 256))
            next_group = _next_group_tile(tile_expert)
            act = _grouped_swiglu_up(hg, moe_w_up, e, tile_expert, tile_valid, next_group,
                                     expert_tile)
            rows = _grouped_matmul(act, moe_w_down, e, tile_expert, tile_valid, next_group,
                                   expert_tile, F32)
            x2, hs = _combine_ln(x2, rows, pos, top_w, gate_ffn1, ln_g[l, 1], ln_b[l, 1],
                                 next_mods, next_dtypes, seq, alpha)
        if hs:
            h = hs[0]
            if len(hs) > 1:
                h_kv = hs[1]
    return x2.reshape(bsz, seq, d)
```

```python
import functools

import jax
import jax.numpy as jnp
from jax import lax
from jax.experimental import pallas as pl
from jax.experimental.pallas import tpu as pltpu

F32 = jnp.float32
BF16 = jnp.bfloat16

LN_EPS = 1e-5
TOP_K = 2
NEG_INF = -1e30
LOG2E = 1.4426950408889634
LANE = 128
FORGET_LANES = LANE
ROUTER_LANES = LANE
BRANCH_DTYPE = BF16
STICK_HEADS_PER_STEP = 8
FOX_HEADS_PER_STEP = 8

V7X_VMEM_BYTES = 64 * 1024 * 1024
VMEM_HEADROOM_BYTES = 12 * 1024 * 1024
VMEM_CAP_BYTES = V7X_VMEM_BYTES - 6 * 1024 * 1024


def _params(semantics, block_bytes):
    limit = min(int(block_bytes) + VMEM_HEADROOM_BYTES, VMEM_CAP_BYTES)
    return pltpu.CompilerParams(dimension_semantics=semantics, vmem_limit_bytes=limit)


def _nbytes(shape, dtype):
    n = 1
    for s in shape:
        n *= s
    return n * jnp.dtype(dtype).itemsize


def _pick(n, prefs):
    for p in prefs:
        if n % p == 0:
            return p
    return n


def _ada_kernel(c_ref, w_ref, b_ref, o_ref):
    c = c_ref[...]
    s = c * jax.nn.sigmoid(c)
    o_ref[...] = jnp.dot(s.astype(BF16), w_ref[...].astype(BF16),
                         preferred_element_type=F32) + b_ref[...]


def _ada_mod(c, ada_w, ada_b):
    bsz, d = c.shape
    n = ada_w.shape[1]
    rows = 8
    c8 = jnp.zeros((rows, d), F32).at[:bsz].set(c)
    tn = _pick(n, (512, 256, 128))
    blk = 2 * (_nbytes((rows, d), F32) + _nbytes((d, tn), F32) + 2 * _nbytes((rows, tn), F32))
    out = pl.pallas_call(
        _ada_kernel,
        grid=(n // tn,),
        in_specs=[pl.BlockSpec((rows, d), lambda j: (0, 0)),
                  pl.BlockSpec((d, tn), lambda j: (0, j)),
                  pl.BlockSpec((1, tn), lambda j: (0, j))],
        out_specs=pl.BlockSpec((rows, tn), lambda j: (0, j)),
        out_shape=jax.ShapeDtypeStruct((rows, n), F32),
        compiler_params=_params(("arbitrary",), blk),
        name="ada_mod",
    )(c8, ada_w, ada_b.reshape(1, n))
    return out[:bsz]


def _modcast_kernel(x_ref, sh_ref, sc_ref, o_ref):
    o_ref[...] = (x_ref[...] * sc_ref[0] + sh_ref[0]).astype(o_ref.dtype)


def _modcast(x2, shift, scale1, seq, out_dtype):
    t, d = x2.shape
    tm = _pick(seq, (256, 128, 64, 32, 16, 8))
    tpb = seq // tm
    vec = pl.BlockSpec((1, 1, d), lambda i: (i // tpb, 0, 0))
    blk = 2 * (_nbytes((tm, d), F32) + _nbytes((tm, d), out_dtype))
    return pl.pallas_call(
        _modcast_kernel,
        grid=(t // tm,),
        in_specs=[pl.BlockSpec((tm, d), lambda i: (i, 0)), vec, vec],
        out_specs=pl.BlockSpec((tm, d), lambda i: (i, 0)),
        out_shape=jax.ShapeDtypeStruct((t, d), out_dtype),
        compiler_params=_params(("arbitrary",), blk),
        name="modcast",
    )(x2, shift, scale1)


def _weight_columns(w_hbm, w_index, col, width):
    src = w_hbm if w_index is None else w_hbm.at[w_index]
    return src.at[:, pl.ds(pl.multiple_of(col, width), width)]


def _stage_weights(w_hbm, w_index, cols, stage_ref, wb_ref, sems, width):
    j = pl.program_id(0)
    n_steps = pl.num_programs(0)

    def copies(jj):
        return [pltpu.make_async_copy(_weight_columns(w_hbm, w_index, c, width),
                                      stage_ref.at[s], sems.at[s])
                for s, c in enumerate(cols(jj))]

    @pl.when(pl.program_id(1) == 0)
    def _():
        @pl.when(j == 0)
        def _():
            for cp in copies(j):
                cp.start()

        for s, cp in enumerate(copies(j)):
            cp.wait()
            wb_ref[s] = stage_ref[s].astype(BF16)

        @pl.when(j + 1 < n_steps)
        def _():
            for cp in copies(j + 1):
                cp.start()


def _mm_kernel(a_ref, w_hbm, o_ref, stage_ref, wb_ref, sems,
               *, w_index, col0, tn, scaled_blocks, col_scale):
    _stage_weights(w_hbm, w_index, lambda jj: [col0 + jj * tn], stage_ref, wb_ref, sems, tn)
    res = jnp.dot(a_ref[...], wb_ref[0], preferred_element_type=F32)
    if scaled_blocks:
        res = res * jnp.where(pl.program_id(0) < scaled_blocks, col_scale, 1.0)
    o_ref[...] = res.astype(o_ref.dtype)


MM_BLOCK_ELEMS = 4 * 1024 * 1024


def _mm_tiles(m, k, n):
    tn = _pick(n, tuple(c for c in (1024, 512, 256, 128) if c * k <= MM_BLOCK_ELEMS))
    tm = _pick(m, tuple(t for t in (1024, 512, 256, 128, 64, 32, 16, 8) if t * k <= MM_BLOCK_ELEMS))
    return tm, tn


def _matmul(a, w, w_index, col0, n, out_dtype, scaled_cols=0, col_scale=1.0):
    m, k = a.shape
    tm, tn = _mm_tiles(m, k, n)
    assert scaled_cols % tn == 0 and col0 % tn == 0
    blk = (2 * (_nbytes((tm, k), BF16) + _nbytes((tm, tn), out_dtype))
           + _nbytes((k, tn), F32) + _nbytes((k, tn), BF16) + _nbytes((tm, tn), F32))
    kern = functools.partial(_mm_kernel, w_index=w_index, col0=col0, tn=tn,
                             scaled_blocks=scaled_cols // tn, col_scale=col_scale)
    return pl.pallas_call(
        kern,
        grid=(n // tn, m // tm),
        in_specs=[pl.BlockSpec((tm, k), lambda j, i: (i, 0)),
                  pl.BlockSpec(memory_space=pl.ANY)],
        out_specs=pl.BlockSpec((tm, tn), lambda j, i: (i, j)),
        out_shape=jax.ShapeDtypeStruct((m, n), out_dtype),
        scratch_shapes=[pltpu.VMEM((1, k, tn), F32), pltpu.VMEM((1, k, tn), BF16),
                        pltpu.SemaphoreType.DMA((1,))],
        compiler_params=_params(("arbitrary", "arbitrary"), blk),
        name="matmul",
    )(a, w)


def _swiglu_kernel(a_ref, w_hbm, o_ref, stage_ref, wb_ref, sems, *, w_index, tn, f):
    _stage_weights(w_hbm, w_index, lambda jj: [jj * tn, f + jj * tn], stage_ref, wb_ref, sems, tn)
    a = a_ref[...]
    g = jnp.dot(a, wb_ref[0], preferred_element_type=F32)
    u = jnp.dot(a, wb_ref[1], preferred_element_type=F32)
    o_ref[...] = (g * jax.nn.sigmoid(g) * u).astype(o_ref.dtype)


def _swiglu_up(a, w_up, w_index):
    m, k = a.shape
    f = w_up.shape[-1] // 2
    tn = _pick(f, tuple(c for c in (512, 256, 128) if 2 * c * k <= MM_BLOCK_ELEMS))
    tm = _pick(m, tuple(t for t in (1024, 512, 256, 128, 64, 32, 16, 8) if t * k <= MM_BLOCK_ELEMS))
    blk = (2 * (_nbytes((tm, k), BF16) + _nbytes((tm, tn), BF16))
           + 2 * (_nbytes((k, tn), F32) + _nbytes((k, tn), BF16)) + 3 * _nbytes((tm, tn), F32))
    return pl.pallas_call(
        functools.partial(_swiglu_kernel, w_index=w_index, tn=tn, f=f),
        grid=(f // tn, m // tm),
        in_specs=[pl.BlockSpec((tm, k), lambda j, i: (i, 0)),
                  pl.BlockSpec(memory_space=pl.ANY)],
        out_specs=pl.BlockSpec((tm, tn), lambda j, i: (i, j)),
        out_shape=jax.ShapeDtypeStruct((m, f), BF16),
        scratch_shapes=[pltpu.VMEM((2, k, tn), F32), pltpu.VMEM((2, k, tn), BF16),
                        pltpu.SemaphoreType.DMA((2,))],
        compiler_params=_params(("arbitrary", "arbitrary"), blk),
        name="swiglu_up",
    )(a, w_up)


def _expert_changed(te_ref, i):
    prev = te_ref[jnp.maximum(i - 1, 0)]
    return jnp.logical_or(i == 0, te_ref[i] != prev)


def _next_group_tile(tile_expert):
    n = tile_expert.shape[0]
    idx = jnp.arange(n, dtype=jnp.int32)
    first = jnp.concatenate([jnp.ones((1,), bool), tile_expert[1:] != tile_expert[:-1]])
    cand = jnp.where(first, idx, n)
    from_here = lax.cummin(cand, axis=0, reverse=True)
    return jnp.concatenate([from_here[1:], jnp.full((1,), n, jnp.int32)])


def _stage_expert_weights(w_hbm, w_index, te_ref, nxt_ref, cols, stage_ref, wb_ref, sems, width):
    j, i = pl.program_id(0), pl.program_id(1)
    n_j, n_i = pl.num_programs(0), pl.num_programs(1)

    def copies(expert, jj):
        src = w_hbm.at[w_index, expert]
        return [pltpu.make_async_copy(src.at[:, pl.ds(pl.multiple_of(c, width), width)],
                                      stage_ref.at[s], sems.at[s])
                for s, c in enumerate(cols(jj))]

    @pl.when(_expert_changed(te_ref, i))
    def _():
        @pl.when(jnp.logical_and(j == 0, i == 0))
        def _():
            for cp in copies(te_ref[0], j):
                cp.start()

        for s, cp in enumerate(copies(te_ref[i], j)):
            cp.wait()
            wb_ref[s] = stage_ref[s].astype(BF16)

        nxt = nxt_ref[i]

        @pl.when(nxt < n_i)
        def _():
            for cp in copies(te_ref[nxt], j):
                cp.start()

        @pl.when(jnp.logical_and(nxt >= n_i, j + 1 < n_j))
        def _():
            for cp in copies(te_ref[0], j + 1):
                cp.start()


def _gswiglu_kernel(te_ref, tv_ref, nxt_ref, a_ref, w_hbm, o_ref, stage_ref, wb_ref, sems,
                    *, w_index, tn, f):
    i = pl.program_id(1)
    _stage_expert_weights(w_hbm, w_index, te_ref, nxt_ref, lambda jj: [jj * tn, f + jj * tn],
                          stage_ref, wb_ref, sems, tn)

    @pl.when(tv_ref[i] > 0)
    def _():
        a = a_ref[...]
        g = jnp.dot(a, wb_ref[0], preferred_element_type=F32)
        u = jnp.dot(a, wb_ref[1], preferred_element_type=F32)
        o_ref[...] = (g * jax.nn.sigmoid(g) * u).astype(o_ref.dtype)

    @pl.when(tv_ref[i] == 0)
    def _():
        o_ref[...] = jnp.zeros_like(o_ref)


def _grouped_swiglu_up(a, w_up, w_index, tile_expert, tile_valid, next_group, tile):
    p, k = a.shape
    f = w_up.shape[-1] // 2
    tn = _pick(f, tuple(c for c in (512, 256, 128) if 2 * c * k <= MM_BLOCK_ELEMS))
    blk = (2 * (_nbytes((tile, k), BF16) + _nbytes((tile, tn), BF16))
           + 2 * (_nbytes((k, tn), F32) + _nbytes((k, tn), BF16)) + 3 * _nbytes((tile, tn), F32))
    grid_spec = pltpu.PrefetchScalarGridSpec(
        num_scalar_prefetch=3,
        grid=(f // tn, p // tile),
        in_specs=[pl.BlockSpec((tile, k), lambda j, i, te, tv, nx: (i, 0)),
                  pl.BlockSpec(memory_space=pl.ANY)],
        out_specs=pl.BlockSpec((tile, tn), lambda j, i, te, tv, nx: (i, j)),
        scratch_shapes=[pltpu.VMEM((2, k, tn), F32), pltpu.VMEM((2, k, tn), BF16),
                        pltpu.SemaphoreType.DMA((2,))],
    )
    return pl.pallas_call(
        functools.partial(_gswiglu_kernel, w_index=w_index, tn=tn, f=f),
        grid_spec=grid_spec,
        out_shape=jax.ShapeDtypeStruct((p, f), BF16),
        compiler_params=_params(("arbitrary", "arbitrary"), blk),
        name="expert_swiglu_up",
    )(tile_expert, tile_valid, next_group, a, w_up)


def _gmm_kernel(te_ref, tv_ref, nxt_ref, a_ref, w_hbm, o_ref, stage_ref, wb_ref, sems,
                *, w_index, tn):
    i = pl.program_id(1)
    _stage_expert_weights(w_hbm, w_index, te_ref, nxt_ref, lambda jj: [jj * tn],
                          stage_ref, wb_ref, sems, tn)

    @pl.when(tv_ref[i] > 0)
    def _():
        o_ref[...] = jnp.dot(a_ref[...], wb_ref[0],
                             preferred_element_type=F32).astype(o_ref.dtype)

    @pl.when(tv_ref[i] == 0)
    def _():
        o_ref[...] = jnp.zeros_like(o_ref)


def _grouped_matmul(a, w, w_index, tile_expert, tile_valid, next_group, tile, out_dtype):
    p, k = a.shape
    n = w.shape[-1]
    tn = _pick(n, tuple(c for c in (2048, 1024, 512, 256, 128) if k * c <= MM_BLOCK_ELEMS))
    blk = (2 * (_nbytes((tile, k), BF16) + _nbytes((tile, tn), out_dtype))
           + _nbytes((k, tn), F32) + _nbytes((k, tn), BF16) + _nbytes((tile, tn), F32))
    grid_spec = pltpu.PrefetchScalarGridSpec(
        num_scalar_prefetch=3,
        grid=(n // tn, p // tile),
        in_specs=[pl.BlockSpec((tile, k), lambda j, i, te, tv, nx: (i, 0)),
                  pl.BlockSpec(memory_space=pl.ANY)],
        out_specs=pl.BlockSpec((tile, tn), lambda j, i, te, tv, nx: (i, j)),
        scratch_shapes=[pltpu.VMEM((1, k, tn), F32), pltpu.VMEM((1, k, tn), BF16),
                        pltpu.SemaphoreType.DMA((1,))],
    )
    return pl.pallas_call(
        functools.partial(_gmm_kernel, w_index=w_index, tn=tn),
        grid_spec=grid_spec,
        out_shape=jax.ShapeDtypeStruct((p, n), out_dtype),
        compiler_params=_params(("arbitrary", "arbitrary"), blk),
        name="expert_matmul",
    )(tile_expert, tile_valid, next_group, a, w)


def _softplus(z):
    return jnp.maximum(z, 0.0) + jnp.log(1.0 + jnp.exp(-jnp.abs(z)))


def _neg_abs(x):
    bits = lax.bitcast_convert_type(x, jnp.uint32) | jnp.uint32(0x80000000)
    return lax.bitcast_convert_type(bits, F32)


def _lanes(x, n):
    reps = n // x.shape[1]
    return x if reps == 1 else jnp.concatenate([x] * reps, axis=1)


def _stick_kernel(q_ref, k_ref, v_ref, o_ref, acc_ref, carry_ref, *, tq, tk, heads, hd):
    qi = pl.program_id(2)
    n_sub = tq // tk
    row = lax.broadcasted_iota(jnp.int32, (tk, tk), 0)
    col = lax.broadcasted_iota(jnp.int32, (tk, tk), 1)
    later = (row > col).astype(BF16)
    acc_ref[...] = jnp.zeros_like(acc_ref)
    carry_ref[...] = jnp.zeros_like(carry_ref)

    def block(g, r0, k0, diag_off):
        nr = tq - r0
        lanes = slice(g * hd, (g + 1) * hd)
        q = q_ref[0, r0:tq, lanes]
        k = k_ref[0, pl.ds(k0, tk), lanes]
        v = v_ref[0, pl.ds(k0, tk), lanes]
        s = lax.dot_general(q, k, (((1,), (1,)), ((), ())), preferred_element_type=F32)
        soft = jnp.log(1.0 + jnp.exp2(_neg_abs(s))) * LOG2E
        log_keep = jnp.minimum(s, 0.0) - soft
        if diag_off is not None:
            q_pos = r0 + lax.broadcasted_iota(jnp.int32, (nr, tk), 0)
            k_pos = diag_off + lax.broadcasted_iota(jnp.int32, (nr, tk), 1)
            mask = k_pos < q_pos
            log_keep = jnp.where(mask, log_keep, 0.0)
        suffix = jnp.dot(log_keep.astype(BF16), later, preferred_element_type=F32)
        carry = carry_ref[g, r0:tq, :]
        w = jnp.exp2((log_keep - s) + suffix + _lanes(carry, tk))
        if diag_off is not None:
            w = jnp.where(mask, w, 0.0)
        acc_ref[g, r0:tq, :] += jnp.dot(w.astype(BF16), v, preferred_element_type=F32)
        carry_ref[g, r0:tq, :] = carry + jnp.sum(log_keep, axis=1, keepdims=True)

    diag0 = qi * tq
    for sub in range(n_sub - 1, -1, -1):
        for g in range(heads):
            block(g, sub * tk, pl.multiple_of(diag0 + sub * tk, tk), sub * tk)

    def body(j, c):
        k0 = pl.multiple_of(diag0 - (j + 1) * tk, tk)
        for g in range(heads):
            block(g, 0, k0, None)
        return c

    lax.fori_loop(0, qi * n_sub, body, 0)
    for g in range(heads):
        o_ref[0, :, g * hd:(g + 1) * hd] = acc_ref[g].astype(o_ref.dtype)


def _attn_tiles(seq):
    tq = _pick(seq, (512, 256, 128))
    tk = min(tq, 256)
    return tq, tk


def _stick_attention(qkv, n_heads, head_dim, out_dtype, heads_per_step):
    bsz, seq, _ = qkv.shape
    d = n_heads * head_dim
    g = heads_per_step
    tq, tk = _attn_tiles(seq)
    width = g * head_dim
    kern = functools.partial(_stick_kernel, tq=tq, tk=tk, heads=g, hd=head_dim)
    blk = (2 * (2 * _nbytes((tq, width), BF16) + 2 * _nbytes((seq, width), BF16))
           + g * (_nbytes((tq, head_dim), F32) + _nbytes((tq, LANE), F32))
           + 8 * g * _nbytes((tq, tk), F32))
    groups = n_heads // g
    return pl.pallas_call(
        kern,
        grid=(bsz, groups, seq // tq),
        in_specs=[pl.BlockSpec((1, tq, width), lambda b, h, i: (b, i, h)),
                  pl.BlockSpec((1, seq, width), lambda b, h, i: (b, 0, groups + h)),
                  pl.BlockSpec((1, seq, width), lambda b, h, i: (b, 0, 2 * groups + h))],
        out_specs=pl.BlockSpec((1, tq, width), lambda b, h, i: (b, i, h)),
        out_shape=jax.ShapeDtypeStruct((bsz, seq, d), out_dtype),
        scratch_shapes=[pltpu.VMEM((g, tq, head_dim), F32), pltpu.VMEM((g, tq, LANE), F32)],
        compiler_params=_params(("arbitrary", "arbitrary", "arbitrary"), blk),
        name="stick_attention",
    )(qkv, qkv, qkv)


def _fox_kernel(q_ref, k_ref, v_ref, fk_ref, o_ref, m_ref, fqb_ref, acc_ref,
                *, tq, tk, tr, heads, hd):
    qi = pl.program_id(2)
    assert tq == tk
    m_ref[...] = jnp.full_like(m_ref, NEG_INF)
    acc_ref[...] = jnp.zeros_like(acc_ref)
    for g in range(heads):
        f_row = fk_ref[0, g, pl.ds(qi, 1), :]
        for c in range(tk // LANE):
            chunk = f_row[:, c * LANE:(c + 1) * LANE]
            fqb_ref[g, c * LANE:(c + 1) * LANE, :] = jnp.broadcast_to(chunk, (LANE, LANE)).T

    def block(g, r0, r1, kb, nk, masked):
        nr = r1 - r0
        lanes = slice(g * hd, (g + 1) * hd)
        k0 = pl.multiple_of(kb * tk, tk)
        q = q_ref[0, r0:r1, lanes]
        k = k_ref[0, pl.ds(k0, nk), lanes]
        v1 = jnp.concatenate([v_ref[0, pl.ds(k0, nk), lanes], jnp.ones((nk, LANE), BF16)], axis=1)
        fk = fk_ref[0, g, pl.ds(kb, 1), :][:, :nk]
        u = lax.dot_general(q, k, (((1,), (1,)), ((), ())), preferred_element_type=F32) - fk
        if masked:
            q_pos = r0 + lax.broadcasted_iota(jnp.int32, (nr, nk), 0)
            k_pos = lax.broadcasted_iota(jnp.int32, (nr, nk), 1)
            u = jnp.where(k_pos <= q_pos, u, NEG_INF)
        fq = fqb_ref[g, r0:r1, :]
        m_prev = m_ref[g, r0:r1, :]
        m_new = jnp.maximum(m_prev, jnp.max(u, axis=1, keepdims=True) + fq)
        alpha = jnp.exp2(m_prev - m_new)
        p = jnp.exp2(u + _lanes(fq - m_new, nk))
        pv = jnp.dot(p.astype(BF16), v1, preferred_element_type=F32)
        acc_ref[g, r0:r1, :] = _lanes(alpha, hd + LANE) * acc_ref[g, r0:r1, :] + pv
        m_ref[g, r0:r1, :] = m_new

    def body(j, c):
        for g in range(heads):
            block(g, 0, tq, j, tk, False)
        return c

    lax.fori_loop(0, qi, body, 0)
    for r0 in range(0, tq, tr):
        for g in range(heads):
            block(g, r0, r0 + tr, qi, r0 + tr, True)
    for g in range(heads):
        acc = acc_ref[g]
        o_ref[0, :, g * hd:(g + 1) * hd] = (acc[:, :hd] / acc[:, hd:]).astype(o_ref.dtype)


def _fox_attention(q, kv, log_f_cum, n_heads, head_dim, out_dtype, heads_per_step):
    bsz, seq, d = q.shape
    g = heads_per_step
    assert head_dim == LANE
    tq = tk = _pick(seq, (512, 256, 128))
    tr = min(tq, 256)
    width = g * head_dim
    groups = n_heads // g
    fk = log_f_cum.reshape(bsz, n_heads, seq // tk, tk)
    kern = functools.partial(_fox_kernel, tq=tq, tk=tk, tr=tr, heads=g, hd=head_dim)
    blk = (2 * (2 * _nbytes((tq, width), BF16) + 2 * _nbytes((seq, width), BF16)
                + g * _nbytes((seq // tk, tk), F32))
           + 4 * g * _nbytes((tq, LANE), F32) + 8 * g * _nbytes((tq, tk), F32))
    return pl.pallas_call(
        kern,
        grid=(bsz, groups, seq // tq),
        in_specs=[pl.BlockSpec((1, tq, width), lambda b, h, i: (b, i, h)),
                  pl.BlockSpec((1, seq, width), lambda b, h, i: (b, 0, h)),
                  pl.BlockSpec((1, seq, width), lambda b, h, i: (b, 0, groups + h)),
                  pl.BlockSpec((1, g, seq // tk, tk), lambda b, h, i: (b, h, 0, 0))],
        out_specs=pl.BlockSpec((1, tq, width), lambda b, h, i: (b, i, h)),
        out_shape=jax.ShapeDtypeStruct((bsz, seq, d), out_dtype),
        scratch_shapes=[pltpu.VMEM((g, tq, LANE), F32), pltpu.VMEM((g, tq, LANE), F32),
                        pltpu.VMEM((g, tq, head_dim + LANE), F32)],
        compiler_params=_params(("arbitrary", "arbitrary", "arbitrary"), blk),
        name="fox_attention",
    )(q, kv, kv, fk)


def _forget_kernel(h_ref, w_ref, b_ref, o_ref, carry_ref, *, ts):
    @pl.when(pl.program_id(1) == 0)
    def _():
        carry_ref[...] = jnp.zeros_like(carry_ref)

    logit = jnp.dot(h_ref[0], w_ref[...].astype(BF16), preferred_element_type=F32) + b_ref[...]
    log_f = -_softplus(-logit) * LOG2E
    row = lax.broadcasted_iota(jnp.int32, (ts, ts), 0)
    col = lax.broadcasted_iota(jnp.int32, (ts, ts), 1)
    upto = (col <= row).astype(F32)
    csum = jnp.dot(upto, log_f, preferred_element_type=F32,
                   precision=lax.Precision.HIGHEST) + carry_ref[...]
    o_ref[0] = csum
    carry_ref[...] = csum[ts - 1:ts, :]


def _forget_cumsum(h3, w_f, b_f):
    bsz, seq, d = h3.shape
    n_heads = w_f.shape[1]
    lanes = FORGET_LANES
    w_pad = jnp.zeros((d, lanes), F32).at[:, :n_heads].set(w_f)
    b_pad = jnp.zeros((1, lanes), F32).at[0, :n_heads].set(b_f.astype(F32))
    ts = _pick(seq, (256, 128, 64, 32, 16, 8))
    blk = 2 * (_nbytes((ts, d), BF16) + _nbytes((d, lanes), F32) + _nbytes((ts, lanes), F32)) \
        + 4 * _nbytes((ts, ts), F32)
    out = pl.pallas_call(
        functools.partial(_forget_kernel, ts=ts),
        grid=(bsz, seq // ts),
        in_specs=[pl.BlockSpec((1, ts, d), lambda b, i: (b, i, 0)),
                  pl.BlockSpec((d, lanes), lambda b, i: (0, 0)),
                  pl.BlockSpec((1, lanes), lambda b, i: (0, 0))],
        out_specs=pl.BlockSpec((1, ts, lanes), lambda b, i: (b, i, 0)),
        out_shape=jax.ShapeDtypeStruct((bsz, seq, lanes), F32),
        scratch_shapes=[pltpu.VMEM((1, lanes), F32)],
        compiler_params=_params(("arbitrary", "arbitrary"), blk),
        name="forget_cumsum",
    )(h3, w_pad, b_pad)
    return out[:, :, :n_heads]


def _ln_rows(v, g, b):
    mu = jnp.mean(v, axis=-1, keepdims=True)
    cen = v - mu
    var = jnp.mean(cen * cen, axis=-1, keepdims=True)
    return cen * lax.rsqrt(var + LN_EPS) * g + b


def _emit_mods(xn, mod_refs, out_refs):
    for m in range(len(out_refs)):
        sh_ref, sc_ref = mod_refs[2 * m], mod_refs[2 * m + 1]
        out_refs[m][...] = (xn * sc_ref[0] + sh_ref[0]).astype(out_refs[m].dtype)


def _split_bf16(v):
    hi = v.astype(BF16)
    return hi, (v - hi.astype(F32)).astype(BF16)


def _route_top2(h, w_ref, b_ref, idx_ref, wgt_ref, n_experts):
    h_hi, h_lo = _split_bf16(h)
    w_hi, w_lo = _split_bf16(w_ref[...])
    logits = (jnp.dot(h_hi, w_hi, preferred_element_type=F32)
              + jnp.dot(h_lo, w_hi, preferred_element_type=F32)
              + jnp.dot(h_hi, w_lo, preferred_element_type=F32)) + b_ref[...]
    lane = lax.broadcasted_iota(jnp.int32, logits.shape, 1)
    logits = jnp.where(lane < n_experts, logits, -jnp.inf)
    big = jnp.int32(ROUTER_LANES)
    v1 = jnp.max(logits, axis=1, keepdims=True)
    i1 = jnp.min(jnp.where(logits == v1, lane, big), axis=1, keepdims=True)
    rest = jnp.where(lane == i1, -jnp.inf, logits)
    v2 = jnp.max(rest, axis=1, keepdims=True)
    i2 = jnp.min(jnp.where(rest == v2, lane, big), axis=1, keepdims=True)
    e2 = jnp.exp(v2 - v1)
    w1 = 1.0 / (1.0 + e2)
    w2 = e2 / (1.0 + e2)
    idx_ref[...] = jnp.where(lane == 0, i1, jnp.where(lane == 1, i2, 0))
    wgt_ref[...] = jnp.where(lane == 0, w1, jnp.where(lane == 1, w2, 0.0))


def _ln_kernel(*refs, alpha, n_mod, n_experts):
    x_ref, y_ref, gate_ref, g_ref, b_ref = refs[:5]
    mod_refs = refs[5:5 + 2 * n_mod]
    n_in = 5 + 2 * n_mod + (2 if n_experts else 0)
    xo_ref = refs[n_in]
    h_refs = refs[n_in + 1:n_in + 1 + n_mod]
    v = alpha * x_ref[...] + gate_ref[0] * y_ref[...].astype(F32)
    xn = _ln_rows(v, g_ref[...], b_ref[...])
    xo_ref[...] = xn
    _emit_mods(xn, mod_refs, h_refs)
    if n_experts:
        wr_ref, br_ref = refs[n_in - 2:n_in]
        idx_ref, wgt_ref = refs[n_in + 1 + n_mod:]
        _route_top2(xn * mod_refs[1][0] + mod_refs[0][0], wr_ref, br_ref, idx_ref, wgt_ref,
                    n_experts)


def _residual_ln(x2, y2, gate1, ln_g, ln_b, mods, mod_dtypes, seq, alpha, router=None):
    t, d = x2.shape
    n_mod = len(mods)
    tm = _pick(seq, (256, 128, 64, 32, 16, 8))
    tpb = seq // tm
    tile = pl.BlockSpec((tm, d), lambda i: (i, 0))
    vec = pl.BlockSpec((1, 1, d), lambda i: (i // tpb, 0, 0))
    par = pl.BlockSpec((1, d), lambda i: (0, 0))
    in_specs = [tile, tile, vec, par, par] + [vec, vec] * n_mod
    args = [x2, y2, gate1, ln_g.reshape(1, d), ln_b.reshape(1, d)]
    for sh, sc1 in mods:
        args += [sh, sc1]
    out_specs = [tile] * (1 + n_mod)
    out_shape = [jax.ShapeDtypeStruct((t, d), F32)] + \
        [jax.ShapeDtypeStruct((t, d), dt) for dt in mod_dtypes]
    n_experts = 0
    if router is not None:
        w_router, b_router = router
        n_experts = w_router.shape[1]
        lanes = ROUTER_LANES
        args += [jnp.zeros((d, lanes), F32).at[:, :n_experts].set(w_router),
                 jnp.zeros((1, lanes), F32).at[0, :n_experts].set(b_router.astype(F32))]
        in_specs += [pl.BlockSpec((d, lanes), lambda i: (0, 0)),
                     pl.BlockSpec((1, lanes), lambda i: (0, 0))]
        out_specs += [pl.BlockSpec((tm, lanes), lambda i: (i, 0))] * 2
        out_shape += [jax.ShapeDtypeStruct((t, lanes), jnp.int32),
                      jax.ShapeDtypeStruct((t, lanes), F32)]
    blk = 2 * (_nbytes((tm, d), F32) + _nbytes((tm, d), y2.dtype) + _nbytes((tm, d), F32)
               + sum(_nbytes((tm, d), dt) for dt in mod_dtypes)) + 6 * _nbytes((tm, d), F32)
    outs = pl.pallas_call(
        functools.partial(_ln_kernel, alpha=alpha, n_mod=n_mod, n_experts=n_experts),
        grid=(t // tm,),
        in_specs=in_specs,
        out_specs=out_specs,
        out_shape=out_shape,
        compiler_params=_params(("arbitrary",), blk),
        name="residual_ln",
    )(*args)
    hs = list(outs[1:1 + n_mod])
    if router is not None:
        return outs[0], hs, outs[-2][:, :TOP_K], outs[-1][:, :TOP_K]
    return outs[0], hs


def _route_plan(top_idx, n_experts, tile):
    t = top_idx.shape[0]
    p = TOP_K * t
    e_flat = top_idx.reshape(p)
    onehot = (e_flat[:, None] == jnp.arange(n_experts, dtype=jnp.int32)[None, :]).astype(jnp.int32)
    csum = jnp.cumsum(onehot, axis=0)
    counts = csum[-1]
    rank = jnp.sum(csum * onehot, axis=1) - 1
    padded = ((counts + tile - 1) // tile) * tile
    gend = jnp.cumsum(padded)
    gstart = gend - padded
    pos = jnp.sum(onehot * gstart[None, :], axis=1) + rank
    n_rows = ((p + tile - 1) // tile + n_experts) * tile
    row_token = jnp.zeros((n_rows,), jnp.int32).at[pos].set(
        jnp.arange(p, dtype=jnp.int32) // TOP_K)
    tile_start = jnp.arange(n_rows // tile, dtype=jnp.int32) * tile
    tile_expert = jnp.sum((tile_start[:, None] >= gend[None, :]).astype(jnp.int32), axis=1)
    tile_valid = (tile_start < gend[-1]).astype(jnp.int32)
    tile_expert = jnp.minimum(tile_expert, n_experts - 1)
    return pos.astype(jnp.int32), row_token, gend[-1:].astype(jnp.int32), tile_expert, tile_valid


def _row_copy(src_ref, src_row, dst_ref, dst_row, sem):
    return pltpu.make_async_copy(src_ref.at[pl.ds(src_row, 1)], dst_ref.at[pl.ds(dst_row, 1)], sem)


GATHER_UNROLL = 8
DMA_PRIORITIES = 2


def _prefetched_gather(issue, n_steps):
    i = pl.program_id(0)
    slot = i % 2

    @pl.when(i == 0)
    def _():
        issue(i, slot)

    @pl.when(i + 1 < n_steps)
    def _():
        issue(i + 1, 1 - slot)

    return slot


def _dispatch_kernel(tok_ref, total_ref, src_ref, o_ref, buf_ref, sems, *, rows, n_steps):
    def issue(step, slot):
        base = step * rows

        @pl.when(base < total_ref[0])
        def _():
            def body(r2, c):
                for k in range(DMA_PRIORITIES):
                    r = r2 * DMA_PRIORITIES + k
                    _row_copy(src_ref, tok_ref[base + r], buf_ref.at[slot], r,
                              sems.at[slot]).start(priority=k)
                return c

            lax.fori_loop(0, rows // DMA_PRIORITIES, body, 0,
                          unroll=min(GATHER_UNROLL // DMA_PRIORITIES, rows // DMA_PRIORITIES))

    slot = _prefetched_gather(issue, n_steps)
    used = pl.program_id(0) * rows < total_ref[0]

    @pl.when(used)
    def _():
        def drain(r, c):
            _row_copy(src_ref, 0, buf_ref.at[slot], r, sems.at[slot]).wait()
            return c

        lax.fori_loop(0, rows, drain, 0, unroll=min(GATHER_UNROLL, rows))
        o_ref[...] = buf_ref[slot].astype(o_ref.dtype)

    @pl.when(jnp.logical_not(used))
    def _():
        o_ref[...] = jnp.zeros_like(o_ref)


def _dispatch(h32, row_token, total_rows, rows):
    t, d = h32.shape
    n_rows = row_token.shape[0]
    n_steps = n_rows // rows
    blk = 2 * _nbytes((rows, d), BF16) + 3 * _nbytes((rows, d), F32)
    grid_spec = pltpu.PrefetchScalarGridSpec(
        num_scalar_prefetch=2,
        grid=(n_steps,),
        in_specs=[pl.BlockSpec(memory_space=pl.ANY)],
        out_specs=pl.BlockSpec((rows, d), lambda i, tok, total: (i, 0)),
        scratch_shapes=[pltpu.VMEM((2, rows, d), F32), pltpu.SemaphoreType.DMA((2,))],
    )
    return pl.pallas_call(
        functools.partial(_dispatch_kernel, rows=rows, n_steps=n_steps),
        grid_spec=grid_spec,
        out_shape=jax.ShapeDtypeStruct((n_rows, d), BF16),
        compiler_params=_params(("arbitrary",), blk),
        name="expert_dispatch",
    )(row_token, total_rows, h32)


def _combine_ln_kernel(*refs, alpha, n_mod, tm, n_steps):
    pos_ref, x_ref, rows_ref, wgt_ref, gate_ref, g_ref, b_ref = refs[:7]
    mod_refs = refs[7:7 + 2 * n_mod]
    xo_ref = refs[7 + 2 * n_mod]
    h_refs = refs[8 + 2 * n_mod:8 + 3 * n_mod]
    buf_ref, sems = refs[8 + 3 * n_mod:]

    def issue(step, slot):
        base = step * tm

        def body(r, c):
            for k in range(TOP_K):
                _row_copy(rows_ref, pos_ref[(base + r) * TOP_K + k], buf_ref.at[slot, k], r,
                          sems.at[slot]).start(priority=k % DMA_PRIORITIES)
            return c

        lax.fori_loop(0, tm, body, 0, unroll=min(GATHER_UNROLL, tm))

    slot = _prefetched_gather(issue, n_steps)

    def drain(r, c):
        for k in range(TOP_K):
            _row_copy(rows_ref, 0, buf_ref.at[slot, k], r, sems.at[slot]).wait()
        return c

    lax.fori_loop(0, tm, drain, 0, unroll=min(GATHER_UNROLL, tm))
    wgt = wgt_ref[...]
    y = wgt[:, 0:1] * buf_ref[slot, 0]
    for k in range(1, TOP_K):
        y = y + wgt[:, k:k + 1] * buf_ref[slot, k]
    v = alpha * x_ref[...] + gate_ref[0] * y
    xn = _ln_rows(v, g_ref[...], b_ref[...])
    xo_ref[...] = xn
    _emit_mods(xn, mod_refs, h_refs)


def _combine_ln(x2, expert_rows, pos, top_w, gate1, ln_g, ln_b, mods, mod_dtypes, seq, alpha):
    t, d = x2.shape
    n_mod = len(mods)
    tm = _pick(seq, (128, 64, 32, 16, 8))
    tpb = seq // tm
    n_steps = t // tm
    tile = pl.BlockSpec((tm, d), lambda i, pos: (i, 0))
    vec = pl.BlockSpec((1, 1, d), lambda i, pos: (i // tpb, 0, 0))
    par = pl.BlockSpec((1, d), lambda i, pos: (0, 0))
    wgt = pl.BlockSpec((tm, TOP_K), lambda i, pos: (i, 0))
    in_specs = [tile, pl.BlockSpec(memory_space=pl.ANY), wgt, vec, par, par] + [vec, vec] * n_mod
    args = [x2, expert_rows, top_w, gate1, ln_g.reshape(1, d), ln_b.reshape(1, d)]
    for sh, sc1 in mods:
        args += [sh, sc1]
    out_shape = [jax.ShapeDtypeStruct((t, d), F32)] + \
        [jax.ShapeDtypeStruct((t, d), dt) for dt in mod_dtypes]
    blk = 2 * (2 * _nbytes((tm, d), F32) + sum(_nbytes((tm, d), dt) for dt in mod_dtypes)) \
        + (2 * TOP_K + 4) * _nbytes((tm, d), F32)
    grid_spec = pltpu.PrefetchScalarGridSpec(
        num_scalar_prefetch=1,
        grid=(n_steps,),
        in_specs=in_specs,
        out_specs=[tile] * (1 + n_mod),
        scratch_shapes=[pltpu.VMEM((2, TOP_K, tm, d), F32), pltpu.SemaphoreType.DMA((2,))],
    )
    outs = pl.pallas_call(
        functools.partial(_combine_ln_kernel, alpha=alpha, n_mod=n_mod, tm=tm, n_steps=n_steps),
        grid_spec=grid_spec,
        out_shape=out_shape,
        compiler_params=_params(("arbitrary",), blk),
        name="combine_ln",
    )(pos, *args)
    return outs[0], list(outs[1:])


def kernel(x, c, ada_w, ada_b, ada_table, kv_table, a_w_qkv, a_w_o, kv_w, kv_b_f, b_w_q, b_w_o,
           ln_g, ln_b, ffn_w_up, ffn_w_down, moe_w_router, moe_b_router, moe_w_up, moe_w_down):
    bsz, seq, d = x.shape
    t = bsz * seq
    depth, n_mod = ada_table.shape[0], ada_table.shape[1]
    n_a = a_w_qkv.shape[0]
    n_heads = kv_w.shape[1] - 2 * d
    head_dim = d // n_heads
    n_experts = moe_w_router.shape[-1]
    alpha = (2.0 * depth) ** 0.25
    expert_tile = _pick(TOP_K * t, (512, 256, 128, 64, 32, 16))
    q_scale = head_dim ** -0.5 * LOG2E
    stick_heads = STICK_HEADS_PER_STEP if n_heads % STICK_HEADS_PER_STEP == 0 else 1
    fox_heads = FOX_HEADS_PER_STEP if n_heads % FOX_HEADS_PER_STEP == 0 else 1

    mod = _ada_mod(c, ada_w, ada_b).reshape(bsz, n_mod, d)

    def vec3(v):
        return v.reshape(bsz, 1, d)

    def layer_mods(l):
        m = mod + ada_table[l][None]
        return [m[:, i] for i in range(n_mod)]

    def mix_mod(l):
        m = layer_mods(l)
        return vec3(m[0]), vec3(1.0 + m[1])

    kv_mod = (vec3(mod[:, 0] + kv_table[0]), vec3(1.0 + mod[:, 1] + kv_table[1]))

    x2 = x.reshape(t, d)
    h = _modcast(x2, *mix_mod(0), seq, BF16)
    h_kv = None
    shared = None
    for l in range(depth):
        m = layer_mods(l)
        gate_mix1, gate_ffn1 = vec3(1.0 + m[2]), vec3(1.0 + m[5])
        ffn_mod = (vec3(m[3]), vec3(1.0 + m[4]))
        is_moe = l % 2 == 1

        if l < n_a:
            qkv = _matmul(h, a_w_qkv, l, 0, 3 * d, BF16, scaled_cols=d, col_scale=-q_scale)
            o = _stick_attention(qkv.reshape(bsz, seq, 3 * d), n_heads, head_dim, BF16,
                                 stick_heads)
            mix = _matmul(o.reshape(t, d), a_w_o, l, 0, d, BRANCH_DTYPE)
        else:
            if shared is None:
                kv = _matmul(h_kv, kv_w.reshape(1, d, kv_w.shape[1]), 0, 0, 2 * d, BF16)
                log_f = _forget_cumsum(h_kv.reshape(bsz, seq, d), kv_w[:, 2 * d:], kv_b_f)
                shared = (kv.reshape(bsz, seq, 2 * d), jnp.transpose(log_f, (0, 2, 1)))
            j = l - n_a
            q = _matmul(h, b_w_q, j, 0, d, BF16, scaled_cols=d, col_scale=q_scale)
            o = _fox_attention(q.reshape(bsz, seq, d), shared[0], shared[1],
                               n_heads, head_dim, BF16, fox_heads)
            mix = _matmul(o.reshape(t, d), b_w_o, j, 0, d, BRANCH_DTYPE)
        if is_moe:
            x2, (h_ffn,), top_idx, top_w = _residual_ln(
                x2, mix, gate_mix1, ln_g[l, 0], ln_b[l, 0], [ffn_mod], [F32], seq, alpha,
                router=(moe_w_router[l // 2], moe_b_router[l // 2]))
        else:
            x2, (h_ffn,) = _residual_ln(x2, mix, gate_mix1, ln_g[l, 0], ln_b[l, 0], [ffn_mod],
                                        [BF16], seq, alpha)

        next_mods, next_dtypes = [], []
        if l + 1 < depth:
            next_mods.append(mix_mod(l + 1))
            next_dtypes.append(BF16)
            if l + 1 == n_a:
                next_mods.append(kv_mod)
                next_dtypes.append(BF16)
        if not is_moe:
            act = _swiglu_up(h_ffn, ffn_w_up, l // 2)
            y = _matmul(act, ffn_w_down, l // 2, 0, d, BRANCH_DTYPE)
            x2, hs = _residual_ln(x2, y, gate_ffn1, ln_g[l, 1], ln_b[l, 1], next_mods,
                                  next_dtypes, seq, alpha)
        else:
            e = l // 2
            pos, row_token, total_rows, tile_expert, tile_valid = _route_plan(
                top_idx, n_experts, expert_tile)
            hg = _dispatch(h_ffn, row_token, total_rows, min(expert_tile, 256))
            next_group = _next_group_tile(tile_expert)
            act = _grouped_swiglu_up(hg, moe_w_up, e, tile_expert, tile_valid, next_group,
                                     expert_tile)
            rows = _grouped_matmul(act, moe_w_down, e, tile_expert, tile_valid, next_group,
                                   expert_tile, F32)
            x2, hs = _combine_ln(x2, rows, pos, top_w, gate_ffn1, ln_g[l, 1], ln_b[l, 1],
                                 next_mods, next_dtypes, seq, alpha)
        if hs:
            h = hs[0]
            if len(hs) > 1:
                h_kv = hs[1]
    return x2.reshape(bsz, seq, d)
```

```python
import functools

import jax
import jax.numpy as jnp
from jax import lax
from jax.experimental import pallas as pl
from jax.experimental.pallas import tpu as pltpu

F32 = jnp.float32
BF16 = jnp.bfloat16

LN_EPS = 1e-5
TOP_K = 2
NEG_INF = -1e30
LOG2E = 1.4426950408889634
LANE = 128
FORGET_LANES = LANE
ROUTER_LANES = LANE
BRANCH_DTYPE = BF16
STICK_HEADS_PER_STEP = 8
FOX_HEADS_PER_STEP = 8

V7X_VMEM_BYTES = 64 * 1024 * 1024
VMEM_HEADROOM_BYTES = 12 * 1024 * 1024
VMEM_CAP_BYTES = V7X_VMEM_BYTES - 6 * 1024 * 1024


def _params(semantics, block_bytes):
    limit = min(int(block_bytes) + VMEM_HEADROOM_BYTES, VMEM_CAP_BYTES)
    return pltpu.CompilerParams(dimension_semantics=semantics, vmem_limit_bytes=limit)


def _nbytes(shape, dtype):
    n = 1
    for s in shape:
        n *= s
    return n * jnp.dtype(dtype).itemsize


def _pick(n, prefs):
    for p in prefs:
        if n % p == 0:
            return p
    return n


def _ada_kernel(c_ref, w_ref, b_ref, o_ref):
    c = c_ref[...]
    s = c * jax.nn.sigmoid(c)
    o_ref[...] = jnp.dot(s.astype(BF16), w_ref[...].astype(BF16),
                         preferred_element_type=F32) + b_ref[...]


def _ada_mod(c, ada_w, ada_b):
    bsz, d = c.shape
    n = ada_w.shape[1]
    rows = 8
    c8 = jnp.zeros((rows, d), F32).at[:bsz].set(c)
    tn = _pick(n, (512, 256, 128))
    blk = 2 * (_nbytes((rows, d), F32) + _nbytes((d, tn), F32) + 2 * _nbytes((rows, tn), F32))
    out = pl.pallas_call(
        _ada_kernel,
        grid=(n // tn,),
        in_specs=[pl.BlockSpec((rows, d), lambda j: (0, 0)),
                  pl.BlockSpec((d, tn), lambda j: (0, j)),
                  pl.BlockSpec((1, tn), lambda j: (0, j))],
        out_specs=pl.BlockSpec((rows, tn), lambda j: (0, j)),
        out_shape=jax.ShapeDtypeStruct((rows, n), F32),
        compiler_params=_params(("arbitrary",), blk),
        name="ada_mod",
    )(c8, ada_w, ada_b.reshape(1, n))
    return out[:bsz]


def _modcast_kernel(x_ref, sh_ref, sc_ref, o_ref):
    o_ref[...] = (x_ref[...] * sc_ref[0] + sh_ref[0]).astype(o_ref.dtype)


def _modcast(x2, shift, scale1, seq, out_dtype):
    t, d = x2.shape
    tm = _pick(seq, (256, 128, 64, 32, 16, 8))
    tpb = seq // tm
    vec = pl.BlockSpec((1, 1, d), lambda i: (i // tpb, 0, 0))
    blk = 2 * (_nbytes((tm, d), F32) + _nbytes((tm, d), out_dtype))
    return pl.pallas_call(
        _modcast_kernel,
        grid=(t // tm,),
        in_specs=[pl.BlockSpec((tm, d), lambda i: (i, 0)), vec, vec],
        out_specs=pl.BlockSpec((tm, d), lambda i: (i, 0)),
        out_shape=jax.ShapeDtypeStruct((t, d), out_dtype),
        compiler_params=_params(("arbitrary",), blk),
        name="modcast",
    )(x2, shift, scale1)


def _weight_columns(w_hbm, w_index, col, width, transposed=False):
    src = w_hbm if w_index is None else w_hbm.at[w_index]
    window = pl.ds(pl.multiple_of(col, width), width)
    return src.at[window] if transposed else src.at[:, window]


def _stage_weights(w_hbm, w_index, cols, stage_ref, wb_ref, sems, width, transposed=False):
    j = pl.program_id(0)
    n_steps = pl.num_programs(0)

    def copies(jj):
        return [pltpu.make_async_copy(_weight_columns(w_hbm, w_index, c, width, transposed),
                                      stage_ref.at[s], sems.at[s])
                for s, c in enumerate(cols(jj))]

    @pl.when(pl.program_id(1) == 0)
    def _():
        @pl.when(j == 0)
        def _():
            for cp in copies(j):
                cp.start()

        for s, cp in enumerate(copies(j)):
            cp.wait()
            wb_ref[s] = stage_ref[s].astype(BF16)

        @pl.when(j + 1 < n_steps)
        def _():
            for cp in copies(j + 1):
                cp.start()


def _mm_kernel(a_ref, w_hbm, o_ref, stage_ref, wb_ref, sems,
               *, w_index, col0, tn, scaled_blocks, col_scale, transposed):
    _stage_weights(w_hbm, w_index, lambda jj: [col0 + jj * tn], stage_ref, wb_ref, sems, tn,
                   transposed)
    contract_w = 1 if transposed else 0
    res = lax.dot_general(a_ref[...], wb_ref[0], (((1,), (contract_w,)), ((), ())),
                          preferred_element_type=F32)
    if scaled_blocks:
        res = res * jnp.where(pl.program_id(0) < scaled_blocks, col_scale, 1.0)
    o_ref[...] = res.astype(o_ref.dtype)


MM_BLOCK_ELEMS = 4 * 1024 * 1024


def _mm_tiles(m, k, n):
    tn = _pick(n, tuple(c for c in (1024, 512, 256, 128) if c * k <= MM_BLOCK_ELEMS))
    tm = _pick(m, tuple(t for t in (1024, 512, 256, 128, 64, 32, 16, 8) if t * k <= MM_BLOCK_ELEMS))
    return tm, tn


def _matmul(a, w, w_index, col0, n, out_dtype, scaled_cols=0, col_scale=1.0, transposed=False):
    m, k = a.shape
    tm, tn = _mm_tiles(m, k, n)
    assert scaled_cols % tn == 0 and col0 % tn == 0
    blk = (2 * (_nbytes((tm, k), BF16) + _nbytes((tm, tn), out_dtype))
           + _nbytes((k, tn), F32) + _nbytes((k, tn), BF16) + _nbytes((tm, tn), F32))
    kern = functools.partial(_mm_kernel, w_index=w_index, col0=col0, tn=tn,
                             scaled_blocks=scaled_cols // tn, col_scale=col_scale,
                             transposed=transposed)
    w_block = (tn, k) if transposed else (k, tn)
    return pl.pallas_call(
        kern,
        grid=(n // tn, m // tm),
        in_specs=[pl.BlockSpec((tm, k), lambda j, i: (i, 0)),
                  pl.BlockSpec(memory_space=pl.ANY)],
        out_specs=pl.BlockSpec((tm, tn), lambda j, i: (i, j)),
        out_shape=jax.ShapeDtypeStruct((m, n), out_dtype),
        scratch_shapes=[pltpu.VMEM((1,) + w_block, F32), pltpu.VMEM((1,) + w_block, BF16),
                        pltpu.SemaphoreType.DMA((1,))],
        compiler_params=_params(("arbitrary", "arbitrary"), blk),
        name="matmul",
    )(a, w)


def _swiglu_kernel(a_ref, w_hbm, o_ref, stage_ref, wb_ref, sems, *, w_index, tn, f):
    _stage_weights(w_hbm, w_index, lambda jj: [jj * tn, f + jj * tn], stage_ref, wb_ref, sems, tn)
    a = a_ref[...]
    g = jnp.dot(a, wb_ref[0], preferred_element_type=F32)
    u = jnp.dot(a, wb_ref[1], preferred_element_type=F32)
    o_ref[...] = (g * jax.nn.sigmoid(g) * u).astype(o_ref.dtype)


def _swiglu_up(a, w_up, w_index):
    m, k = a.shape
    f = w_up.shape[-1] // 2
    tn = _pick(f, tuple(c for c in (512, 256, 128) if 2 * c * k <= MM_BLOCK_ELEMS))
    tm = _pick(m, tuple(t for t in (1024, 512, 256, 128, 64, 32, 16, 8) if t * k <= MM_BLOCK_ELEMS))
    blk = (2 * (_nbytes((tm, k), BF16) + _nbytes((tm, tn), BF16))
           + 2 * (_nbytes((k, tn), F32) + _nbytes((k, tn), BF16)) + 3 * _nbytes((tm, tn), F32))
    return pl.pallas_call(
        functools.partial(_swiglu_kernel, w_index=w_index, tn=tn, f=f),
        grid=(f // tn, m // tm),
        in_specs=[pl.BlockSpec((tm, k), lambda j, i: (i, 0)),
                  pl.BlockSpec(memory_space=pl.ANY)],
        out_specs=pl.BlockSpec((tm, tn), lambda j, i: (i, j)),
        out_shape=jax.ShapeDtypeStruct((m, f), BF16),
        scratch_shapes=[pltpu.VMEM((2, k, tn), F32), pltpu.VMEM((2, k, tn), BF16),
                        pltpu.SemaphoreType.DMA((2,))],
        compiler_params=_params(("arbitrary", "arbitrary"), blk),
        name="swiglu_up",
    )(a, w_up)


def _expert_changed(te_ref, i):
    prev = te_ref[jnp.maximum(i - 1, 0)]
    return jnp.logical_or(i == 0, te_ref[i] != prev)


def _next_group_tile(tile_expert):
    n = tile_expert.shape[0]
    idx = jnp.arange(n, dtype=jnp.int32)
    first = jnp.concatenate([jnp.ones((1,), bool), tile_expert[1:] != tile_expert[:-1]])
    cand = jnp.where(first, idx, n)
    from_here = lax.cummin(cand, axis=0, reverse=True)
    return jnp.concatenate([from_here[1:], jnp.full((1,), n, jnp.int32)])


def _stage_expert_weights(w_hbm, w_index, te_ref, nxt_ref, cols, stage_ref, wb_ref, sems, width):
    j, i = pl.program_id(0), pl.program_id(1)
    n_j, n_i = pl.num_programs(0), pl.num_programs(1)

    def copies(expert, jj):
        src = w_hbm.at[w_index, expert]
        return [pltpu.make_async_copy(src.at[:, pl.ds(pl.multiple_of(c, width), width)],
                                      stage_ref.at[s], sems.at[s])
                for s, c in enumerate(cols(jj))]

    @pl.when(_expert_changed(te_ref, i))
    def _():
        @pl.when(jnp.logical_and(j == 0, i == 0))
        def _():
            for cp in copies(te_ref[0], j):
                cp.start()

        for s, cp in enumerate(copies(te_ref[i], j)):
            cp.wait()
            wb_ref[s] = stage_ref[s].astype(BF16)

        nxt = nxt_ref[i]

        @pl.when(nxt < n_i)
        def _():
            for cp in copies(te_ref[nxt], j):
                cp.start()

        @pl.when(jnp.logical_and(nxt >= n_i, j + 1 < n_j))
        def _():
            for cp in copies(te_ref[0], j + 1):
                cp.start()


def _gswiglu_kernel(te_ref, tv_ref, nxt_ref, a_ref, w_hbm, o_ref, stage_ref, wb_ref, sems,
                    *, w_index, tn, f):
    i = pl.program_id(1)
    _stage_expert_weights(w_hbm, w_index, te_ref, nxt_ref, lambda jj: [jj * tn, f + jj * tn],
                          stage_ref, wb_ref, sems, tn)

    @pl.when(tv_ref[i] > 0)
    def _():
        a = a_ref[...]
        g = jnp.dot(a, wb_ref[0], preferred_element_type=F32)
        u = jnp.dot(a, wb_ref[1], preferred_element_type=F32)
        o_ref[...] = (g * jax.nn.sigmoid(g) * u).astype(o_ref.dtype)

    @pl.when(tv_ref[i] == 0)
    def _():
        o_ref[...] = jnp.zeros_like(o_ref)


def _grouped_swiglu_up(a, w_up, w_index, tile_expert, tile_valid, next_group, tile):
    p, k = a.shape
    f = w_up.shape[-1] // 2
    tn = _pick(f, tuple(c for c in (512, 256, 128) if 2 * c * k <= MM_BLOCK_ELEMS))
    blk = (2 * (_nbytes((tile, k), BF16) + _nbytes((tile, tn), BF16))
           + 2 * (_nbytes((k, tn), F32) + _nbytes((k, tn), BF16)) + 3 * _nbytes((tile, tn), F32))
    grid_spec = pltpu.PrefetchScalarGridSpec(
        num_scalar_prefetch=3,
        grid=(f // tn, p // tile),
        in_specs=[pl.BlockSpec((tile, k), lambda j, i, te, tv, nx: (i, 0)),
                  pl.BlockSpec(memory_space=pl.ANY)],
        out_specs=pl.BlockSpec((tile, tn), lambda j, i, te, tv, nx: (i, j)),
        scratch_shapes=[pltpu.VMEM((2, k, tn), F32), pltpu.VMEM((2, k, tn), BF16),
                        pltpu.SemaphoreType.DMA((2,))],
    )
    return pl.pallas_call(
        functools.partial(_gswiglu_kernel, w_index=w_index, tn=tn, f=f),
        grid_spec=grid_spec,
        out_shape=jax.ShapeDtypeStruct((p, f), BF16),
        compiler_params=_params(("arbitrary", "arbitrary"), blk),
        name="expert_swiglu_up",
    )(tile_expert, tile_valid, next_group, a, w_up)


def _gmm_kernel(te_ref, tv_ref, nxt_ref, a_ref, w_hbm, o_ref, stage_ref, wb_ref, sems,
                *, w_index, tn):
    i = pl.program_id(1)
    _stage_expert_weights(w_hbm, w_index, te_ref, nxt_ref, lambda jj: [jj * tn],
                          stage_ref, wb_ref, sems, tn)

    @pl.when(tv_ref[i] > 0)
    def _():
        o_ref[...] = jnp.dot(a_ref[...], wb_ref[0],
                             preferred_element_type=F32).astype(o_ref.dtype)

    @pl.when(tv_ref[i] == 0)
    def _():
        o_ref[...] = jnp.zeros_like(o_ref)


def _grouped_matmul(a, w, w_index, tile_expert, tile_valid, next_group, tile, out_dtype):
    p, k = a.shape
    n = w.shape[-1]
    tn = _pick(n, tuple(c for c in (2048, 1024, 512, 256, 128) if k * c <= MM_BLOCK_ELEMS))
    blk = (2 * (_nbytes((tile, k), BF16) + _nbytes((tile, tn), out_dtype))
           + _nbytes((k, tn), F32) + _nbytes((k, tn), BF16) + _nbytes((tile, tn), F32))
    grid_spec = pltpu.PrefetchScalarGridSpec(
        num_scalar_prefetch=3,
        grid=(n // tn, p // tile),
        in_specs=[pl.BlockSpec((tile, k), lambda j, i, te, tv, nx: (i, 0)),
                  pl.BlockSpec(memory_space=pl.ANY)],
        out_specs=pl.BlockSpec((tile, tn), lambda j, i, te, tv, nx: (i, j)),
        scratch_shapes=[pltpu.VMEM((1, k, tn), F32), pltpu.VMEM((1, k, tn), BF16),
                        pltpu.SemaphoreType.DMA((1,))],
    )
    return pl.pallas_call(
        functools.partial(_gmm_kernel, w_index=w_index, tn=tn),
        grid_spec=grid_spec,
        out_shape=jax.ShapeDtypeStruct((p, n), out_dtype),
        compiler_params=_params(("arbitrary", "arbitrary"), blk),
        name="expert_matmul",
    )(tile_expert, tile_valid, next_group, a, w)


def _softplus(z):
    return jnp.maximum(z, 0.0) + jnp.log(1.0 + jnp.exp(-jnp.abs(z)))


def _neg_abs(x):
    bits = lax.bitcast_convert_type(x, jnp.uint32) | jnp.uint32(0x80000000)
    return lax.bitcast_convert_type(bits, F32)


def _lanes(x, n):
    reps = n // x.shape[1]
    return x if reps == 1 else jnp.concatenate([x] * reps, axis=1)


def _stick_kernel(q_ref, k_ref, v_ref, o_ref, acc_ref, carry_ref, *, tq, tk, heads, hd):
    qi = pl.program_id(2)
    n_sub = tq // tk
    row = lax.broadcasted_iota(jnp.int32, (tk, tk), 0)
    col = lax.broadcasted_iota(jnp.int32, (tk, tk), 1)
    later = (row > col).astype(BF16)
    acc_ref[...] = jnp.zeros_like(acc_ref)
    carry_ref[...] = jnp.zeros_like(carry_ref)

    def block(g, r0, k0, diag_off):
        nr = tq - r0
        lanes = slice(g * hd, (g + 1) * hd)
        q = q_ref[0, r0:tq, lanes]
        k = k_ref[0, pl.ds(k0, tk), lanes]
        v = v_ref[0, pl.ds(k0, tk), lanes]
        s = lax.dot_general(q, k, (((1,), (1,)), ((), ())), preferred_element_type=F32)
        soft = jnp.log(1.0 + jnp.exp2(_neg_abs(s))) * LOG2E
        log_keep = jnp.minimum(s, 0.0) - soft
        if diag_off is not None:
            q_pos = r0 + lax.broadcasted_iota(jnp.int32, (nr, tk), 0)
            k_pos = diag_off + lax.broadcasted_iota(jnp.int32, (nr, tk), 1)
            mask = k_pos < q_pos
            log_keep = jnp.where(mask, log_keep, 0.0)
        suffix = jnp.dot(log_keep.astype(BF16), later, preferred_element_type=F32)
        carry = carry_ref[g, r0:tq, :]
        w = jnp.exp2((log_keep - s) + suffix + _lanes(carry, tk))
        if diag_off is not None:
            w = jnp.where(mask, w, 0.0)
        acc_ref[g, r0:tq, :] += jnp.dot(w.astype(BF16), v, preferred_element_type=F32)
        carry_ref[g, r0:tq, :] = carry + jnp.sum(log_keep, axis=1, keepdims=True)

    diag0 = qi * tq
    for sub in range(n_sub - 1, -1, -1):
        for g in range(heads):
            block(g, sub * tk, pl.multiple_of(diag0 + sub * tk, tk), sub * tk)

    def body(j, c):
        k0 = pl.multiple_of(diag0 - (j + 1) * tk, tk)
        for g in range(heads):
            block(g, 0, k0, None)
        return c

    lax.fori_loop(0, qi * n_sub, body, 0)
    for g in range(heads):
        o_ref[0, :, g * hd:(g + 1) * hd] = acc_ref[g].astype(o_ref.dtype)


def _attn_tiles(seq):
    tq = _pick(seq, (512, 256, 128))
    tk = min(tq, 256)
    return tq, tk


def _stick_attention(qkv, n_heads, head_dim, out_dtype, heads_per_step):
    bsz, seq, _ = qkv.shape
    d = n_heads * head_dim
    g = heads_per_step
    tq, tk = _attn_tiles(seq)
    width = g * head_dim
    kern = functools.partial(_stick_kernel, tq=tq, tk=tk, heads=g, hd=head_dim)
    blk = (2 * (2 * _nbytes((tq, width), BF16) + 2 * _nbytes((seq, width), BF16))
           + g * (_nbytes((tq, head_dim), F32) + _nbytes((tq, LANE), F32))
           + 8 * g * _nbytes((tq, tk), F32))
    groups = n_heads // g
    return pl.pallas_call(
        kern,
        grid=(bsz, groups, seq // tq),
        in_specs=[pl.BlockSpec((1, tq, width), lambda b, h, i: (b, i, h)),
                  pl.BlockSpec((1, seq, width), lambda b, h, i: (b, 0, groups + h)),
                  pl.BlockSpec((1, seq, width), lambda b, h, i: (b, 0, 2 * groups + h))],
        out_specs=pl.BlockSpec((1, tq, width), lambda b, h, i: (b, i, h)),
        out_shape=jax.ShapeDtypeStruct((bsz, seq, d), out_dtype),
        scratch_shapes=[pltpu.VMEM((g, tq, head_dim), F32), pltpu.VMEM((g, tq, LANE), F32)],
        compiler_params=_params(("arbitrary", "arbitrary", "arbitrary"), blk),
        name="stick_attention",
    )(qkv, qkv, qkv)


def _fox_kernel(q_ref, k_ref, v_ref, fk_ref, o_ref, m_ref, fqb_ref, acc_ref,
                *, tq, tk, tr, heads, hd):
    qi = pl.program_id(2)
    assert tq == tk
    m_ref[...] = jnp.full_like(m_ref, NEG_INF)
    acc_ref[...] = jnp.zeros_like(acc_ref)
    for g in range(heads):
        f_row = fk_ref[0, g, pl.ds(qi, 1), :]
        for c in range(tk // LANE):
            chunk = f_row[:, c * LANE:(c + 1) * LANE]
            fqb_ref[g, c * LANE:(c + 1) * LANE, :] = jnp.broadcast_to(chunk, (LANE, LANE)).T

    def block(g, r0, r1, kb, nk, masked):
        nr = r1 - r0
        lanes = slice(g * hd, (g + 1) * hd)
        k0 = pl.multiple_of(kb * tk, tk)
        q = q_ref[0, r0:r1, lanes]
        k = k_ref[0, pl.ds(k0, nk), lanes]
        v1 = jnp.concatenate([v_ref[0, pl.ds(k0, nk), lanes], jnp.ones((nk, LANE), BF16)], axis=1)
        fk = fk_ref[0, g, pl.ds(kb, 1), :][:, :nk]
        u = lax.dot_general(q, k, (((1,), (1,)), ((), ())), preferred_element_type=F32) - fk
        if masked:
            q_pos = r0 + lax.broadcasted_iota(jnp.int32, (nr, nk), 0)
            k_pos = lax.broadcasted_iota(jnp.int32, (nr, nk), 1)
            u = jnp.where(k_pos <= q_pos, u, NEG_INF)
        fq = fqb_ref[g, r0:r1, :]
        m_prev = m_ref[g, r0:r1, :]
        m_new = jnp.maximum(m_prev, jnp.max(u, axis=1, keepdims=True) + fq)
        alpha = jnp.exp2(m_prev - m_new)
        p = jnp.exp2(u + _lanes(fq - m_new, nk))
        pv = jnp.dot(p.astype(BF16), v1, preferred_element_type=F32)
        acc_ref[g, r0:r1, :] = _lanes(alpha, hd + LANE) * acc_ref[g, r0:r1, :] + pv
        m_ref[g, r0:r1, :] = m_new

    def body(j, c):
        for g in range(heads):
            block(g, 0, tq, j, tk, False)
        return c

    lax.fori_loop(0, qi, body, 0)
    for r0 in range(0, tq, tr):
        for g in range(heads):
            block(g, r0, r0 + tr, qi, r0 + tr, True)
    for g in range(heads):
        acc = acc_ref[g]
        o_ref[0, :, g * hd:(g + 1) * hd] = (acc[:, :hd] / acc[:, hd:]).astype(o_ref.dtype)


def _fox_attention(q, kv, log_f_cum, n_heads, head_dim, out_dtype, heads_per_step):
    bsz, seq, d = q.shape
    g = heads_per_step
    assert head_dim == LANE
    tq = tk = _pick(seq, (512, 256, 128))
    tr = min(tq, 256)
    width = g * head_dim
    groups = n_heads // g
    fk = log_f_cum.reshape(bsz, n_heads, seq // tk, tk)
    kern = functools.partial(_fox_kernel, tq=tq, tk=tk, tr=tr, heads=g, hd=head_dim)
    blk = (2 * (2 * _nbytes((tq, width), BF16) + 2 * _nbytes((seq, width), BF16)
                + g * _nbytes((seq // tk, tk), F32))
           + 4 * g * _nbytes((tq, LANE), F32) + 8 * g * _nbytes((tq, tk), F32))
    return pl.pallas_call(
        kern,
        grid=(bsz, groups, seq // tq),
        in_specs=[pl.BlockSpec((1, tq, width), lambda b, h, i: (b, i, h)),
                  pl.BlockSpec((1, seq, width), lambda b, h, i: (b, 0, h)),
                  pl.BlockSpec((1, seq, width), lambda b, h, i: (b, 0, groups + h)),
                  pl.BlockSpec((1, g, seq // tk, tk), lambda b, h, i: (b, h, 0, 0))],
        out_specs=pl.BlockSpec((1, tq, width), lambda b, h, i: (b, i, h)),
        out_shape=jax.ShapeDtypeStruct((bsz, seq, d), out_dtype),
        scratch_shapes=[pltpu.VMEM((g, tq, LANE), F32), pltpu.VMEM((g, tq, LANE), F32),
                        pltpu.VMEM((g, tq, head_dim + LANE), F32)],
        compiler_params=_params(("arbitrary", "arbitrary", "arbitrary"), blk),
        name="fox_attention",
    )(q, kv, kv, fk)


def _forget_kernel(h_ref, w_ref, b_ref, o_ref, carry_ref, *, ts):
    @pl.when(pl.program_id(1) == 0)
    def _():
        carry_ref[...] = jnp.zeros_like(carry_ref)

    logit = jnp.dot(h_ref[0], w_ref[...].astype(BF16), preferred_element_type=F32) + b_ref[...]
    log_f = -_softplus(-logit) * LOG2E
    row = lax.broadcasted_iota(jnp.int32, (ts, ts), 0)
    col = lax.broadcasted_iota(jnp.int32, (ts, ts), 1)
    upto = (col <= row).astype(F32)
    csum = jnp.dot(upto, log_f, preferred_element_type=F32,
                   precision=lax.Precision.HIGHEST) + carry_ref[...]
    o_ref[0] = csum
    carry_ref[...] = csum[ts - 1:ts, :]


def _forget_cumsum(h3, w_f, b_f):
    bsz, seq, d = h3.shape
    n_heads = w_f.shape[1]
    lanes = FORGET_LANES
    w_pad = jnp.zeros((d, lanes), F32).at[:, :n_heads].set(w_f)
    b_pad = jnp.zeros((1, lanes), F32).at[0, :n_heads].set(b_f.astype(F32))
    ts = _pick(seq, (256, 128, 64, 32, 16, 8))
    blk = 2 * (_nbytes((ts, d), BF16) + _nbytes((d, lanes), F32) + _nbytes((ts, lanes), F32)) \
        + 4 * _nbytes((ts, ts), F32)
    out = pl.pallas_call(
        functools.partial(_forget_kernel, ts=ts),
        grid=(bsz, seq // ts),
        in_specs=[pl.BlockSpec((1, ts, d), lambda b, i: (b, i, 0)),
                  pl.BlockSpec((d, lanes), lambda b, i: (0, 0)),
                  pl.BlockSpec((1, lanes), lambda b, i: (0, 0))],
        out_specs=pl.BlockSpec((1, ts, lanes), lambda b, i: (b, i, 0)),
        out_shape=jax.ShapeDtypeStruct((bsz, seq, lanes), F32),
        scratch_shapes=[pltpu.VMEM((1, lanes), F32)],
        compiler_params=_params(("arbitrary", "arbitrary"), blk),
        name="forget_cumsum",
    )(h3, w_pad, b_pad)
    return out[:, :, :n_heads]


def _ln_rows(v, g, b):
    mu = jnp.mean(v, axis=-1, keepdims=True)
    cen = v - mu
    var = jnp.mean(cen * cen, axis=-1, keepdims=True)
    return cen * lax.rsqrt(var + LN_EPS) * g + b


PACKED = jnp.uint32


def _pack_halves(h):
    half = h.shape[1] // 2
    lo = lax.bitcast_convert_type(h[:, :half].astype(BF16).astype(F32), jnp.uint32)
    hi = lax.bitcast_convert_type(h[:, half:].astype(BF16).astype(F32), jnp.uint32)
    return hi | (lo >> 16)


def _unpack_halves(u):
    lo = lax.bitcast_convert_type(u << 16, F32).astype(BF16)
    hi = lax.bitcast_convert_type(u & jnp.uint32(0xFFFF0000), F32).astype(BF16)
    return jnp.concatenate([lo, hi], axis=1)


def _emit_mods(xn, mod_refs, out_refs):
    for m in range(len(out_refs)):
        sh_ref, sc_ref = mod_refs[2 * m], mod_refs[2 * m + 1]
        h = xn * sc_ref[0] + sh_ref[0]
        if out_refs[m].dtype == PACKED:
            out_refs[m][...] = _pack_halves(h)
        else:
            out_refs[m][...] = h.astype(out_refs[m].dtype)


def _split_bf16(v):
    hi = v.astype(BF16)
    return hi, (v - hi.astype(F32)).astype(BF16)


def _route_top2(h, w_ref, b_ref, idx_ref, wgt_ref, n_experts):
    h_hi, h_lo = _split_bf16(h)
    w_hi, w_lo = _split_bf16(w_ref[...])
    logits = (jnp.dot(h_hi, w_hi, preferred_element_type=F32)
              + jnp.dot(h_lo, w_hi, preferred_element_type=F32)
              + jnp.dot(h_hi, w_lo, preferred_element_type=F32)) + b_ref[...]
    lane = lax.broadcasted_iota(jnp.int32, logits.shape, 1)
    logits = jnp.where(lane < n_experts, logits, -jnp.inf)
    big = jnp.int32(ROUTER_LANES)
    v1 = jnp.max(logits, axis=1, keepdims=True)
    i1 = jnp.min(jnp.where(logits == v1, lane, big), axis=1, keepdims=True)
    rest = jnp.where(lane == i1, -jnp.inf, logits)
    v2 = jnp.max(rest, axis=1, keepdims=True)
    i2 = jnp.min(jnp.where(rest == v2, lane, big), axis=1, keepdims=True)
    e2 = jnp.exp(v2 - v1)
    w1 = 1.0 / (1.0 + e2)
    w2 = e2 / (1.0 + e2)
    idx_ref[...] = jnp.where(lane == 0, i1, jnp.where(lane == 1, i2, 0))
    wgt_ref[...] = jnp.where(lane == 0, w1, jnp.where(lane == 1, w2, 0.0))


def _ln_kernel(*refs, alpha, n_mod, n_experts):
    x_ref, y_ref, gate_ref, g_ref, b_ref = refs[:5]
    mod_refs = refs[5:5 + 2 * n_mod]
    n_in = 5 + 2 * n_mod + (2 if n_experts else 0)
    xo_ref = refs[n_in]
    h_refs = refs[n_in + 1:n_in + 1 + n_mod]
    v = alpha * x_ref[...] + gate_ref[0] * y_ref[...].astype(F32)
    xn = _ln_rows(v, g_ref[...], b_ref[...])
    xo_ref[...] = xn
    _emit_mods(xn, mod_refs, h_refs)
    if n_experts:
        wr_ref, br_ref = refs[n_in - 2:n_in]
        idx_ref, wgt_ref = refs[n_in + 1 + n_mod:]
        _route_top2(xn * mod_refs[1][0] + mod_refs[0][0], wr_ref, br_ref, idx_ref, wgt_ref,
                    n_experts)


def _residual_ln(x2, y2, gate1, ln_g, ln_b, mods, mod_dtypes, seq, alpha, router=None):
    t, d = x2.shape
    n_mod = len(mods)
    tm = _pick(seq, (256, 128, 64, 32, 16, 8))
    tpb = seq // tm
    tile = pl.BlockSpec((tm, d), lambda i: (i, 0))
    vec = pl.BlockSpec((1, 1, d), lambda i: (i // tpb, 0, 0))
    par = pl.BlockSpec((1, d), lambda i: (0, 0))
    in_specs = [tile, tile, vec, par, par] + [vec, vec] * n_mod
    args = [x2, y2, gate1, ln_g.reshape(1, d), ln_b.reshape(1, d)]
    for sh, sc1 in mods:
        args += [sh, sc1]
    def width(dt):
        return d // 2 if dt == PACKED else d

    out_specs = [tile] + [pl.BlockSpec((tm, width(dt)), lambda i: (i, 0)) for dt in mod_dtypes]
    out_shape = [jax.ShapeDtypeStruct((t, d), F32)] + \
        [jax.ShapeDtypeStruct((t, width(dt)), dt) for dt in mod_dtypes]
    n_experts = 0
    if router is not None:
        w_router, b_router = router
        n_experts = w_router.shape[1]
        lanes = ROUTER_LANES
        args += [jnp.zeros((d, lanes), F32).at[:, :n_experts].set(w_router),
                 jnp.zeros((1, lanes), F32).at[0, :n_experts].set(b_router.astype(F32))]
        in_specs += [pl.BlockSpec((d, lanes), lambda i: (0, 0)),
                     pl.BlockSpec((1, lanes), lambda i: (0, 0))]
        out_specs += [pl.BlockSpec((tm, lanes), lambda i: (i, 0))] * 2
        out_shape += [jax.ShapeDtypeStruct((t, lanes), jnp.int32),
                      jax.ShapeDtypeStruct((t, lanes), F32)]
    blk = 2 * (_nbytes((tm, d), F32) + _nbytes((tm, d), y2.dtype) + _nbytes((tm, d), F32)
               + sum(_nbytes((tm, d), dt) for dt in mod_dtypes)) + 6 * _nbytes((tm, d), F32)
    outs = pl.pallas_call(
        functools.partial(_ln_kernel, alpha=alpha, n_mod=n_mod, n_experts=n_experts),
        grid=(t // tm,),
        in_specs=in_specs,
        out_specs=out_specs,
        out_shape=out_shape,
        compiler_params=_params(("arbitrary",), blk),
        name="residual_ln",
    )(*args)
    hs = list(outs[1:1 + n_mod])
    if router is not None:
        return outs[0], hs, outs[-2][:, :TOP_K], outs[-1][:, :TOP_K]
    return outs[0], hs


def _route_plan(top_idx, n_experts, tile):
    t = top_idx.shape[0]
    p = TOP_K * t
    e_flat = top_idx.reshape(p)
    onehot = (e_flat[:, None] == jnp.arange(n_experts, dtype=jnp.int32)[None, :]).astype(jnp.int32)
    csum = jnp.cumsum(onehot, axis=0)
    counts = csum[-1]
    rank = jnp.sum(csum * onehot, axis=1) - 1
    padded = ((counts + tile - 1) // tile) * tile
    gend = jnp.cumsum(padded)
    gstart = gend - padded
    pos = jnp.sum(onehot * gstart[None, :], axis=1) + rank
    n_rows = ((p + tile - 1) // tile + n_experts) * tile
    row_token = jnp.zeros((n_rows,), jnp.int32).at[pos].set(
        jnp.arange(p, dtype=jnp.int32) // TOP_K)
    tile_start = jnp.arange(n_rows // tile, dtype=jnp.int32) * tile
    tile_expert = jnp.sum((tile_start[:, None] >= gend[None, :]).astype(jnp.int32), axis=1)
    tile_valid = (tile_start < gend[-1]).astype(jnp.int32)
    tile_expert = jnp.minimum(tile_expert, n_experts - 1)
    return pos.astype(jnp.int32), row_token, gend[-1:].astype(jnp.int32), tile_expert, tile_valid


def _row_copy(src_ref, src_row, dst_ref, dst_row, sem):
    return pltpu.make_async_copy(src_ref.at[pl.ds(src_row, 1)], dst_ref.at[pl.ds(dst_row, 1)], sem)


GATHER_UNROLL = 8
DMA_PRIORITIES = 2


def _prefetched_gather(issue, n_steps):
    i = pl.program_id(0)
    slot = i % 2

    @pl.when(i == 0)
    def _():
        issue(i, slot)

    @pl.when(i + 1 < n_steps)
    def _():
        issue(i + 1, 1 - slot)

    return slot


def _dispatch_kernel(tok_ref, total_ref, src_ref, o_ref, buf_ref, sems, *, rows, n_steps):
    def issue(step, slot):
        base = step * rows

        @pl.when(base < total_ref[0])
        def _():
            def body(r2, c):
                for k in range(DMA_PRIORITIES):
                    r = r2 * DMA_PRIORITIES + k
                    _row_copy(src_ref, tok_ref[base + r], buf_ref.at[slot], r,
                              sems.at[slot]).start(priority=k)
                return c

            lax.fori_loop(0, rows // DMA_PRIORITIES, body, 0,
                          unroll=min(GATHER_UNROLL // DMA_PRIORITIES, rows // DMA_PRIORITIES))

    slot = _prefetched_gather(issue, n_steps)
    used = pl.program_id(0) * rows < total_ref[0]

    @pl.when(used)
    def _():
        def drain(r, c):
            _row_copy(src_ref, 0, buf_ref.at[slot], r, sems.at[slot]).wait()
            return c

        lax.fori_loop(0, rows, drain, 0, unroll=min(GATHER_UNROLL, rows))
        o_ref[...] = _unpack_halves(buf_ref[slot])

    @pl.when(jnp.logical_not(used))
    def _():
        o_ref[...] = jnp.zeros_like(o_ref)


def _dispatch(h_packed, row_token, total_rows, rows):
    t, half = h_packed.shape
    d = 2 * half
    n_rows = row_token.shape[0]
    n_steps = n_rows // rows
    blk = 2 * _nbytes((rows, d), BF16) + 4 * _nbytes((rows, half), PACKED)
    grid_spec = pltpu.PrefetchScalarGridSpec(
        num_scalar_prefetch=2,
        grid=(n_steps,),
        in_specs=[pl.BlockSpec(memory_space=pl.ANY)],
        out_specs=pl.BlockSpec((rows, d), lambda i, tok, total: (i, 0)),
        scratch_shapes=[pltpu.VMEM((2, rows, half), PACKED), pltpu.SemaphoreType.DMA((2,))],
    )
    return pl.pallas_call(
        functools.partial(_dispatch_kernel, rows=rows, n_steps=n_steps),
        grid_spec=grid_spec,
        out_shape=jax.ShapeDtypeStruct((n_rows, d), BF16),
        compiler_params=_params(("arbitrary",), blk),
        name="expert_dispatch",
    )(row_token, total_rows, h_packed)


def _combine_ln_kernel(*refs, alpha, n_mod, tm, n_steps):
    pos_ref, x_ref, rows_ref, wgt_ref, gate_ref, g_ref, b_ref = refs[:7]
    mod_refs = refs[7:7 + 2 * n_mod]
    xo_ref = refs[7 + 2 * n_mod]
    h_refs = refs[8 + 2 * n_mod:8 + 3 * n_mod]
    buf_ref, sems = refs[8 + 3 * n_mod:]

    def issue(step, slot):
        base = step * tm

        def body(r, c):
            for k in range(TOP_K):
                _row_copy(rows_ref, pos_ref[(base + r) * TOP_K + k], buf_ref.at[slot, k], r,
                          sems.at[slot]).start(priority=k % DMA_PRIORITIES)
            return c

        lax.fori_loop(0, tm, body, 0, unroll=min(GATHER_UNROLL, tm))

    slot = _prefetched_gather(issue, n_steps)

    def drain(r, c):
        for k in range(TOP_K):
            _row_copy(rows_ref, 0, buf_ref.at[slot, k], r, sems.at[slot]).wait()
        return c

    lax.fori_loop(0, tm, drain, 0, unroll=min(GATHER_UNROLL, tm))
    wgt = wgt_ref[...]
    y = wgt[:, 0:1] * buf_ref[slot, 0]
    for k in range(1, TOP_K):
        y = y + wgt[:, k:k + 1] * buf_ref[slot, k]
    v = alpha * x_ref[...] + gate_ref[0] * y
    xn = _ln_rows(v, g_ref[...], b_ref[...])
    xo_ref[...] = xn
    _emit_mods(xn, mod_refs, h_refs)


def _combine_ln(x2, expert_rows, pos, top_w, gate1, ln_g, ln_b, mods, mod_dtypes, seq, alpha):
    t, d = x2.shape
    n_mod = len(mods)
    tm = _pick(seq, (128, 64, 32, 16, 8))
    tpb = seq // tm
    n_steps = t // tm
    tile = pl.BlockSpec((tm, d), lambda i, pos: (i, 0))
    vec = pl.BlockSpec((1, 1, d), lambda i, pos: (i // tpb, 0, 0))
    par = pl.BlockSpec((1, d), lambda i, pos: (0, 0))
    wgt = pl.BlockSpec((tm, TOP_K), lambda i, pos: (i, 0))
    in_specs = [tile, pl.BlockSpec(memory_space=pl.ANY), wgt, vec, par, par] + [vec, vec] * n_mod
    args = [x2, expert_rows, top_w, gate1, ln_g.reshape(1, d), ln_b.reshape(1, d)]
    for sh, sc1 in mods:
        args += [sh, sc1]
    out_shape = [jax.ShapeDtypeStruct((t, d), F32)] + \
        [jax.ShapeDtypeStruct((t, d), dt) for dt in mod_dtypes]
    blk = 2 * (2 * _nbytes((tm, d), F32) + sum(_nbytes((tm, d), dt) for dt in mod_dtypes)) \
        + (2 * TOP_K + 4) * _nbytes((tm, d), F32)
    grid_spec = pltpu.PrefetchScalarGridSpec(
        num_scalar_prefetch=1,
        grid=(n_steps,),
        in_specs=in_specs,
        out_specs=[tile] * (1 + n_mod),
        scratch_shapes=[pltpu.VMEM((2, TOP_K, tm, d), F32), pltpu.SemaphoreType.DMA((2,))],
    )
    outs = pl.pallas_call(
        functools.partial(_combine_ln_kernel, alpha=alpha, n_mod=n_mod, tm=tm, n_steps=n_steps),
        grid_spec=grid_spec,
        out_shape=out_shape,
        compiler_params=_params(("arbitrary",), blk),
        name="combine_ln",
    )(pos, *args)
    return outs[0], list(outs[1:])


def kernel(x, c, ada_w, ada_b, ada_table, kv_table, a_w_qkv, a_w_o, kv_w, kv_b_f, b_w_q, b_w_o,
           ln_g, ln_b, ffn_w_up, ffn_w_down, moe_w_router, moe_b_router, moe_w_up, moe_w_down):
    bsz, seq, d = x.shape
    t = bsz * seq
    depth, n_mod = ada_table.shape[0], ada_table.shape[1]
    n_a = a_w_qkv.shape[0]
    n_heads = kv_w.shape[1] - 2 * d
    head_dim = d // n_heads
    n_experts = moe_w_router.shape[-1]
    alpha = (2.0 * depth) ** 0.25
    expert_tile = _pick(TOP_K * t, (512, 256, 128, 64, 32, 16))
    q_scale = head_dim ** -0.5 * LOG2E
    stick_heads = STICK_HEADS_PER_STEP if n_heads % STICK_HEADS_PER_STEP == 0 else 1
    fox_heads = FOX_HEADS_PER_STEP if n_heads % FOX_HEADS_PER_STEP == 0 else 1

    mod = _ada_mod(c, ada_w, ada_b).reshape(bsz, n_mod, d)

    def vec3(v):
        return v.reshape(bsz, 1, d)

    def layer_mods(l):
        m = mod + ada_table[l][None]
        return [m[:, i] for i in range(n_mod)]

    def mix_mod(l):
        m = layer_mods(l)
        return vec3(m[0]), vec3(1.0 + m[1])

    kv_mod = (vec3(mod[:, 0] + kv_table[0]), vec3(1.0 + mod[:, 1] + kv_table[1]))

    x2 = x.reshape(t, d)
    h = _modcast(x2, *mix_mod(0), seq, BF16)
    h_kv = None
    shared = None
    for l in range(depth):
        m = layer_mods(l)
        gate_mix1, gate_ffn1 = vec3(1.0 + m[2]), vec3(1.0 + m[5])
        ffn_mod = (vec3(m[3]), vec3(1.0 + m[4]))
        is_moe = l % 2 == 1

        if l < n_a:
            qkv = _matmul(h, a_w_qkv, l, 0, 3 * d, BF16, scaled_cols=d, col_scale=-q_scale)
            o = _stick_attention(qkv.reshape(bsz, seq, 3 * d), n_heads, head_dim, BF16,
                                 stick_heads)
            mix = _matmul(o.reshape(t, d), a_w_o, l, 0, d, BRANCH_DTYPE)
        else:
            if shared is None:
                kv = _matmul(h_kv, kv_w.T, None, 0, 2 * d, BF16, transposed=True)
                log_f = _forget_cumsum(h_kv.reshape(bsz, seq, d), kv_w[:, 2 * d:], kv_b_f)
                shared = (kv.reshape(bsz, seq, 2 * d), jnp.transpose(log_f, (0, 2, 1)))
            j = l - n_a
            q = _matmul(h, b_w_q, j, 0, d, BF16, scaled_cols=d, col_scale=q_scale)
            o = _fox_attention(q.reshape(bsz, seq, d), shared[0], shared[1],
                               n_heads, head_dim, BF16, fox_heads)
            mix = _matmul(o.reshape(t, d), b_w_o, j, 0, d, BRANCH_DTYPE)
        if is_moe:
            x2, (h_ffn,), top_idx, top_w = _residual_ln(
                x2, mix, gate_mix1, ln_g[l, 0], ln_b[l, 0], [ffn_mod], [PACKED], seq, alpha,
                router=(moe_w_router[l // 2], moe_b_router[l // 2]))
        else:
            x2, (h_ffn,) = _residual_ln(x2, mix, gate_mix1, ln_g[l, 0], ln_b[l, 0], [ffn_mod],
                                        [BF16], seq, alpha)

        next_mods, next_dtypes = [], []
        if l + 1 < depth:
            next_mods.append(mix_mod(l + 1))
            next_dtypes.append(BF16)
            if l + 1 == n_a:
                next_mods.append(kv_mod)
                next_dtypes.append(BF16)
        if not is_moe:
            act = _swiglu_up(h_ffn, ffn_w_up, l // 2)
            y = _matmul(act, ffn_w_down, l // 2, 0, d, BRANCH_DTYPE)
            x2, hs = _residual_ln(x2, y, gate_ffn1, ln_g[l, 1], ln_b[l, 1], next_mods,
                                  next_dtypes, seq, alpha)
        else:
            e = l // 2
            pos, row_token, total_rows, tile_expert, tile_valid = _route_plan(
                top_idx, n_experts, expert_tile)
            hg = _dispatch(h_ffn, row_token, total_rows, min(expert_tile, 256))
            next_group = _next_group_tile(tile_expert)
            act = _grouped_swiglu_up(hg, moe_w_up, e, tile_expert, tile_valid, next_group,
                                     expert_tile)
            rows = _grouped_matmul(act, moe_w_down, e, tile_expert, tile_valid, next_group,
                                   expert_tile, F32)
            x2, hs = _combine_ln(x2, rows, pos, top_w, gate_ffn1, ln_g[l, 1], ln_b[l, 1],
                                 next_mods, next_dtypes, seq, alpha)
        if hs:
            h = hs[0]
            if len(hs) > 1:
                h_kv = hs[1]
    return x2.reshape(bsz, seq, d)
```

```python
import functools

import jax
import jax.numpy as jnp
from jax import lax
from jax.experimental import pallas as pl
from jax.experimental.pallas import tpu as pltpu

F32 = jnp.float32
BF16 = jnp.bfloat16

LN_EPS = 1e-5
TOP_K = 2
NEG_INF = -1e30
LOG2E = 1.4426950408889634
LANE = 128
FORGET_LANES = LANE
ROUTER_LANES = LANE
BRANCH_DTYPE = BF16
STICK_HEADS_PER_STEP = 8
FOX_HEADS_PER_STEP = 8

V7X_VMEM_BYTES = 64 * 1024 * 1024
VMEM_HEADROOM_BYTES = 12 * 1024 * 1024
VMEM_CAP_BYTES = V7X_VMEM_BYTES - 6 * 1024 * 1024


def _params(semantics, block_bytes):
    limit = min(int(block_bytes) + VMEM_HEADROOM_BYTES, VMEM_CAP_BYTES)
    return pltpu.CompilerParams(dimension_semantics=semantics, vmem_limit_bytes=limit)


def _nbytes(shape, dtype):
    n = 1
    for s in shape:
        n *= s
    return n * jnp.dtype(dtype).itemsize


def _pick(n, prefs):
    for p in prefs:
        if n % p == 0:
            return p
    return n


def _ada_kernel(c_ref, w_ref, b_ref, o_ref):
    c = c_ref[...]
    s = c * jax.nn.sigmoid(c)
    o_ref[...] = jnp.dot(s.astype(BF16), w_ref[...].astype(BF16),
                         preferred_element_type=F32) + b_ref[...]


def _ada_mod(c, ada_w, ada_b):
    bsz, d = c.shape
    n = ada_w.shape[1]
    rows = 8
    c8 = jnp.zeros((rows, d), F32).at[:bsz].set(c)
    tn = _pick(n, (512, 256, 128))
    blk = 2 * (_nbytes((rows, d), F32) + _nbytes((d, tn), F32) + 2 * _nbytes((rows, tn), F32))
    out = pl.pallas_call(
        _ada_kernel,
        grid=(n // tn,),
        in_specs=[pl.BlockSpec((rows, d), lambda j: (0, 0)),
                  pl.BlockSpec((d, tn), lambda j: (0, j)),
                  pl.BlockSpec((1, tn), lambda j: (0, j))],
        out_specs=pl.BlockSpec((rows, tn), lambda j: (0, j)),
        out_shape=jax.ShapeDtypeStruct((rows, n), F32),
        compiler_params=_params(("arbitrary",), blk),
        name="ada_mod",
    )(c8, ada_w, ada_b.reshape(1, n))
    return out[:bsz]


def _modcast_kernel(x_ref, sh_ref, sc_ref, o_ref):
    o_ref[...] = (x_ref[...] * sc_ref[0] + sh_ref[0]).astype(o_ref.dtype)


def _modcast(x2, shift, scale1, seq, out_dtype):
    t, d = x2.shape
    tm = _pick(seq, (256, 128, 64, 32, 16, 8))
    tpb = seq // tm
    vec = pl.BlockSpec((1, 1, d), lambda i: (i // tpb, 0, 0))
    blk = 2 * (_nbytes((tm, d), F32) + _nbytes((tm, d), out_dtype))
    return pl.pallas_call(
        _modcast_kernel,
        grid=(t // tm,),
        in_specs=[pl.BlockSpec((tm, d), lambda i: (i, 0)), vec, vec],
        out_specs=pl.BlockSpec((tm, d), lambda i: (i, 0)),
        out_shape=jax.ShapeDtypeStruct((t, d), out_dtype),
        compiler_params=_params(("arbitrary",), blk),
        name="modcast",
    )(x2, shift, scale1)


def _weight_columns(w_hbm, w_index, col, width, transposed=False):
    src = w_hbm if w_index is None else w_hbm.at[w_index]
    window = pl.ds(pl.multiple_of(col, width), width)
    return src.at[window] if transposed else src.at[:, window]


def _stage_weights(w_hbm, w_index, cols, stage_ref, wb_ref, sems, width, transposed=False):
    j = pl.program_id(0)
    n_steps = pl.num_programs(0)

    def copies(jj):
        return [pltpu.make_async_copy(_weight_columns(w_hbm, w_index, c, width, transposed),
                                      stage_ref.at[s], sems.at[s])
                for s, c in enumerate(cols(jj))]

    @pl.when(pl.program_id(1) == 0)
    def _():
        @pl.when(j == 0)
        def _():
            for cp in copies(j):
                cp.start()

        for s, cp in enumerate(copies(j)):
            cp.wait()
            wb_ref[s] = stage_ref[s].astype(BF16)

        @pl.when(j + 1 < n_steps)
        def _():
            for cp in copies(j + 1):
                cp.start()


def _mm_kernel(a_ref, w_hbm, o_ref, stage_ref, wb_ref, sems,
               *, w_index, col0, tn, scaled_blocks, col_scale, transposed):
    _stage_weights(w_hbm, w_index, lambda jj: [col0 + jj * tn], stage_ref, wb_ref, sems, tn,
                   transposed)
    contract_w = 1 if transposed else 0
    res = lax.dot_general(a_ref[...], wb_ref[0], (((1,), (contract_w,)), ((), ())),
                          preferred_element_type=F32)
    if scaled_blocks:
        res = res * jnp.where(pl.program_id(0) < scaled_blocks, col_scale, 1.0)
    o_ref[...] = res.astype(o_ref.dtype)


MM_BLOCK_ELEMS = 4 * 1024 * 1024


def _mm_tiles(m, k, n):
    tn = _pick(n, tuple(c for c in (1024, 512, 256, 128) if c * k <= MM_BLOCK_ELEMS))
    tm = _pick(m, tuple(t for t in (1024, 512, 256, 128, 64, 32, 16, 8) if t * k <= MM_BLOCK_ELEMS))
    return tm, tn


def _matmul(a, w, w_index, col0, n, out_dtype, scaled_cols=0, col_scale=1.0, transposed=False):
    m, k = a.shape
    tm, tn = _mm_tiles(m, k, n)
    assert scaled_cols % tn == 0 and col0 % tn == 0
    blk = (2 * (_nbytes((tm, k), BF16) + _nbytes((tm, tn), out_dtype))
           + _nbytes((k, tn), F32) + _nbytes((k, tn), BF16) + _nbytes((tm, tn), F32))
    kern = functools.partial(_mm_kernel, w_index=w_index, col0=col0, tn=tn,
                             scaled_blocks=scaled_cols // tn, col_scale=col_scale,
                             transposed=transposed)
    w_block = (tn, k) if transposed else (k, tn)
    return pl.pallas_call(
        kern,
        grid=(n // tn, m // tm),
        in_specs=[pl.BlockSpec((tm, k), lambda j, i: (i, 0)),
                  pl.BlockSpec(memory_space=pl.ANY)],
        out_specs=pl.BlockSpec((tm, tn), lambda j, i: (i, j)),
        out_shape=jax.ShapeDtypeStruct((m, n), out_dtype),
        scratch_shapes=[pltpu.VMEM((1,) + w_block, F32), pltpu.VMEM((1,) + w_block, BF16),
                        pltpu.SemaphoreType.DMA((1,))],
        compiler_params=_params(("arbitrary", "arbitrary"), blk),
        name="matmul",
    )(a, w)


def _swiglu_kernel(a_ref, w_hbm, o_ref, stage_ref, wb_ref, sems, *, w_index, tn, f):
    _stage_weights(w_hbm, w_index, lambda jj: [jj * tn, f + jj * tn], stage_ref, wb_ref, sems, tn)
    a = a_ref[...]
    g = jnp.dot(a, wb_ref[0], preferred_element_type=F32)
    u = jnp.dot(a, wb_ref[1], preferred_element_type=F32)
    o_ref[...] = (g * jax.nn.sigmoid(g) * u).astype(o_ref.dtype)


def _swiglu_up(a, w_up, w_index):
    m, k = a.shape
    f = w_up.shape[-1] // 2
    tn = _pick(f, tuple(c for c in (512, 256, 128) if 2 * c * k <= MM_BLOCK_ELEMS))
    tm = _pick(m, tuple(t for t in (1024, 512, 256, 128, 64, 32, 16, 8) if t * k <= MM_BLOCK_ELEMS))
    blk = (2 * (_nbytes((tm, k), BF16) + _nbytes((tm, tn), BF16))
           + 2 * (_nbytes((k, tn), F32) + _nbytes((k, tn), BF16)) + 3 * _nbytes((tm, tn), F32))
    return pl.pallas_call(
        functools.partial(_swiglu_kernel, w_index=w_index, tn=tn, f=f),
        grid=(f // tn, m // tm),
        in_specs=[pl.BlockSpec((tm, k), lambda j, i: (i, 0)),
                  pl.BlockSpec(memory_space=pl.ANY)],
        out_specs=pl.BlockSpec((tm, tn), lambda j, i: (i, j)),
        out_shape=jax.ShapeDtypeStruct((m, f), BF16),
        scratch_shapes=[pltpu.VMEM((2, k, tn), F32), pltpu.VMEM((2, k, tn), BF16),
                        pltpu.SemaphoreType.DMA((2,))],
        compiler_params=_params(("arbitrary", "arbitrary"), blk),
        name="swiglu_up",
    )(a, w_up)


def _expert_changed(te_ref, i):
    prev = te_ref[jnp.maximum(i - 1, 0)]
    return jnp.logical_or(i == 0, te_ref[i] != prev)


def _next_group_tile(tile_expert):
    n = tile_expert.shape[0]
    idx = jnp.arange(n, dtype=jnp.int32)
    first = jnp.concatenate([jnp.ones((1,), bool), tile_expert[1:] != tile_expert[:-1]])
    cand = jnp.where(first, idx, n)
    from_here = lax.cummin(cand, axis=0, reverse=True)
    return jnp.concatenate([from_here[1:], jnp.full((1,), n, jnp.int32)])


def _stage_expert_weights(w_hbm, w_index, te_ref, nxt_ref, cols, stage_ref, wb_ref, sems, width):
    j, i = pl.program_id(0), pl.program_id(1)
    n_j, n_i = pl.num_programs(0), pl.num_programs(1)

    def copies(expert, jj):
        src = w_hbm.at[w_index, expert]
        return [pltpu.make_async_copy(src.at[:, pl.ds(pl.multiple_of(c, width), width)],
                                      stage_ref.at[s], sems.at[s])
                for s, c in enumerate(cols(jj))]

    @pl.when(_expert_changed(te_ref, i))
    def _():
        @pl.when(jnp.logical_and(j == 0, i == 0))
        def _():
            for cp in copies(te_ref[0], j):
                cp.start()

        for s, cp in enumerate(copies(te_ref[i], j)):
            cp.wait()
            wb_ref[s] = stage_ref[s].astype(BF16)

        nxt = nxt_ref[i]

        @pl.when(nxt < n_i)
        def _():
            for cp in copies(te_ref[nxt], j):
                cp.start()

        @pl.when(jnp.logical_and(nxt >= n_i, j + 1 < n_j))
        def _():
            for cp in copies(te_ref[0], j + 1):
                cp.start()


def _gswiglu_kernel(te_ref, tv_ref, nxt_ref, a_ref, w_hbm, o_ref, stage_ref, wb_ref, sems,
                    *, w_index, tn, f):
    i = pl.program_id(1)
    _stage_expert_weights(w_hbm, w_index, te_ref, nxt_ref, lambda jj: [jj * tn, f + jj * tn],
                          stage_ref, wb_ref, sems, tn)

    @pl.when(tv_ref[i] > 0)
    def _():
        a = a_ref[...]
        g = jnp.dot(a, wb_ref[0], preferred_element_type=F32)
        u = jnp.dot(a, wb_ref[1], preferred_element_type=F32)
        o_ref[...] = (g * jax.nn.sigmoid(g) * u).astype(o_ref.dtype)

    @pl.when(tv_ref[i] == 0)
    def _():
        o_ref[...] = jnp.zeros_like(o_ref)


def _grouped_swiglu_up(a, w_up, w_index, tile_expert, tile_valid, next_group, tile):
    p, k = a.shape
    f = w_up.shape[-1] // 2
    tn = _pick(f, tuple(c for c in (512, 256, 128) if 2 * c * k <= MM_BLOCK_ELEMS))
    blk = (2 * (_nbytes((tile, k), BF16) + _nbytes((tile, tn), BF16))
           + 2 * (_nbytes((k, tn), F32) + _nbytes((k, tn), BF16)) + 3 * _nbytes((tile, tn), F32))
    grid_spec = pltpu.PrefetchScalarGridSpec(
        num_scalar_prefetch=3,
        grid=(f // tn, p // tile),
        in_specs=[pl.BlockSpec((tile, k), lambda j, i, te, tv, nx: (i, 0)),
                  pl.BlockSpec(memory_space=pl.ANY)],
        out_specs=pl.BlockSpec((tile, tn), lambda j, i, te, tv, nx: (i, j)),
        scratch_shapes=[pltpu.VMEM((2, k, tn), F32), pltpu.VMEM((2, k, tn), BF16),
                        pltpu.SemaphoreType.DMA((2,))],
    )
    return pl.pallas_call(
        functools.partial(_gswiglu_kernel, w_index=w_index, tn=tn, f=f),
        grid_spec=grid_spec,
        out_shape=jax.ShapeDtypeStruct((p, f), BF16),
        compiler_params=_params(("arbitrary", "arbitrary"), blk),
        name="expert_swiglu_up",
    )(tile_expert, tile_valid, next_group, a, w_up)


def _gmm_kernel(te_ref, tv_ref, nxt_ref, a_ref, w_hbm, o_ref, stage_ref, wb_ref, sems,
                *, w_index, tn):
    i = pl.program_id(1)
    _stage_expert_weights(w_hbm, w_index, te_ref, nxt_ref, lambda jj: [jj * tn],
                          stage_ref, wb_ref, sems, tn)

    @pl.when(tv_ref[i] > 0)
    def _():
        o_ref[...] = _pack_halves(jnp.dot(a_ref[...], wb_ref[0], preferred_element_type=F32))

    @pl.when(tv_ref[i] == 0)
    def _():
        o_ref[...] = jnp.zeros_like(o_ref)


def _grouped_matmul(a, w, w_index, tile_expert, tile_valid, next_group, tile):
    p, k = a.shape
    n = w.shape[-1]
    tn = _pick(n, tuple(c for c in (2048, 1024, 512, 256) if k * c <= MM_BLOCK_ELEMS))
    blk = (2 * (_nbytes((tile, k), BF16) + _nbytes((tile, tn // 2), PACKED))
           + _nbytes((k, tn), F32) + _nbytes((k, tn), BF16) + 2 * _nbytes((tile, tn), F32))
    grid_spec = pltpu.PrefetchScalarGridSpec(
        num_scalar_prefetch=3,
        grid=(n // tn, p // tile),
        in_specs=[pl.BlockSpec((tile, k), lambda j, i, te, tv, nx: (i, 0)),
                  pl.BlockSpec(memory_space=pl.ANY)],
        out_specs=pl.BlockSpec((tile, tn // 2), lambda j, i, te, tv, nx: (i, j)),
        scratch_shapes=[pltpu.VMEM((1, k, tn), F32), pltpu.VMEM((1, k, tn), BF16),
                        pltpu.SemaphoreType.DMA((1,))],
    )
    rows = pl.pallas_call(
        functools.partial(_gmm_kernel, w_index=w_index, tn=tn),
        grid_spec=grid_spec,
        out_shape=jax.ShapeDtypeStruct((p, n // 2), PACKED),
        compiler_params=_params(("arbitrary", "arbitrary"), blk),
        name="expert_matmul",
    )(tile_expert, tile_valid, next_group, a, w)
    return rows, tn


def _softplus(z):
    return jnp.maximum(z, 0.0) + jnp.log(1.0 + jnp.exp(-jnp.abs(z)))


def _neg_abs(x):
    bits = lax.bitcast_convert_type(x, jnp.uint32) | jnp.uint32(0x80000000)
    return lax.bitcast_convert_type(bits, F32)


def _lanes(x, n):
    reps = n // x.shape[1]
    return x if reps == 1 else jnp.concatenate([x] * reps, axis=1)


def _stick_kernel(q_ref, k_ref, v_ref, o_ref, acc_ref, carry_ref, *, tq, tk, heads, hd):
    qi = pl.program_id(2)
    n_sub = tq // tk
    row = lax.broadcasted_iota(jnp.int32, (tk, tk), 0)
    col = lax.broadcasted_iota(jnp.int32, (tk, tk), 1)
    later = (row > col).astype(BF16)
    acc_ref[...] = jnp.zeros_like(acc_ref)
    carry_ref[...] = jnp.zeros_like(carry_ref)

    def block(g, r0, k0, diag_off):
        nr = tq - r0
        lanes = slice(g * hd, (g + 1) * hd)
        q = q_ref[0, r0:tq, lanes]
        k = k_ref[0, pl.ds(k0, tk), lanes]
        v = v_ref[0, pl.ds(k0, tk), lanes]
        s = lax.dot_general(q, k, (((1,), (1,)), ((), ())), preferred_element_type=F32)
        soft = jnp.log(1.0 + jnp.exp2(_neg_abs(s))) * LOG2E
        log_keep = jnp.minimum(s, 0.0) - soft
        if diag_off is not None:
            q_pos = r0 + lax.broadcasted_iota(jnp.int32, (nr, tk), 0)
            k_pos = diag_off + lax.broadcasted_iota(jnp.int32, (nr, tk), 1)
            mask = k_pos < q_pos
            log_keep = jnp.where(mask, log_keep, 0.0)
        suffix = jnp.dot(log_keep.astype(BF16), later, preferred_element_type=F32)
        carry = carry_ref[g, r0:tq, :]
        w = jnp.exp2((log_keep - s) + suffix + _lanes(carry, tk))
        if diag_off is not None:
            w = jnp.where(mask, w, 0.0)
        acc_ref[g, r0:tq, :] += jnp.dot(w.astype(BF16), v, preferred_element_type=F32)
        carry_ref[g, r0:tq, :] = carry + jnp.sum(log_keep, axis=1, keepdims=True)

    diag0 = qi * tq
    for sub in range(n_sub - 1, -1, -1):
        for g in range(heads):
            block(g, sub * tk, pl.multiple_of(diag0 + sub * tk, tk), sub * tk)

    def body(j, c):
        k0 = pl.multiple_of(diag0 - (j + 1) * tk, tk)
        for g in range(heads):
            block(g, 0, k0, None)
        return c

    lax.fori_loop(0, qi * n_sub, body, 0)
    for g in range(heads):
        o_ref[0, :, g * hd:(g + 1) * hd] = acc_ref[g].astype(o_ref.dtype)


def _attn_tiles(seq):
    tq = _pick(seq, (512, 256, 128))
    tk = min(tq, 256)
    return tq, tk


def _stick_attention(qkv, n_heads, head_dim, out_dtype, heads_per_step):
    bsz, seq, _ = qkv.shape
    d = n_heads * head_dim
    g = heads_per_step
    tq, tk = _attn_tiles(seq)
    width = g * head_dim
    kern = functools.partial(_stick_kernel, tq=tq, tk=tk, heads=g, hd=head_dim)
    blk = (2 * (2 * _nbytes((tq, width), BF16) + 2 * _nbytes((seq, width), BF16))
           + g * (_nbytes((tq, head_dim), F32) + _nbytes((tq, LANE), F32))
           + 8 * g * _nbytes((tq, tk), F32))
    groups = n_heads // g
    return pl.pallas_call(
        kern,
        grid=(bsz, groups, seq // tq),
        in_specs=[pl.BlockSpec((1, tq, width), lambda b, h, i: (b, i, h)),
                  pl.BlockSpec((1, seq, width), lambda b, h, i: (b, 0, groups + h)),
                  pl.BlockSpec((1, seq, width), lambda b, h, i: (b, 0, 2 * groups + h))],
        out_specs=pl.BlockSpec((1, tq, width), lambda b, h, i: (b, i, h)),
        out_shape=jax.ShapeDtypeStruct((bsz, seq, d), out_dtype),
        scratch_shapes=[pltpu.VMEM((g, tq, head_dim), F32), pltpu.VMEM((g, tq, LANE), F32)],
        compiler_params=_params(("arbitrary", "arbitrary", "arbitrary"), blk),
        name="stick_attention",
    )(qkv, qkv, qkv)


def _fox_kernel(q_ref, k_ref, v_ref, fk_ref, o_ref, m_ref, fqb_ref, acc_ref,
                *, tq, tk, tr, heads, hd):
    qi = pl.program_id(2)
    assert tq == tk
    m_ref[...] = jnp.full_like(m_ref, NEG_INF)
    acc_ref[...] = jnp.zeros_like(acc_ref)
    for g in range(heads):
        f_row = fk_ref[0, g, pl.ds(qi, 1), :]
        for c in range(tk // LANE):
            chunk = f_row[:, c * LANE:(c + 1) * LANE]
            fqb_ref[g, c * LANE:(c + 1) * LANE, :] = jnp.broadcast_to(chunk, (LANE, LANE)).T

    def block(g, r0, r1, kb, nk, masked):
        nr = r1 - r0
        lanes = slice(g * hd, (g + 1) * hd)
        k0 = pl.multiple_of(kb * tk, tk)
        q = q_ref[0, r0:r1, lanes]
        k = k_ref[0, pl.ds(k0, nk), lanes]
        v1 = jnp.concatenate([v_ref[0, pl.ds(k0, nk), lanes], jnp.ones((nk, LANE), BF16)], axis=1)
        fk = fk_ref[0, g, pl.ds(kb, 1), :][:, :nk]
        u = lax.dot_general(q, k, (((1,), (1,)), ((), ())), preferred_element_type=F32) - fk
        if masked:
            q_pos = r0 + lax.broadcasted_iota(jnp.int32, (nr, nk), 0)
            k_pos = lax.broadcasted_iota(jnp.int32, (nr, nk), 1)
            u = jnp.where(k_pos <= q_pos, u, NEG_INF)
        fq = fqb_ref[g, r0:r1, :]
        m_prev = m_ref[g, r0:r1, :]
        m_new = jnp.maximum(m_prev, jnp.max(u, axis=1, keepdims=True) + fq)
        alpha = jnp.exp2(m_prev - m_new)
        p = jnp.exp2(u + _lanes(fq - m_new, nk))
        pv = jnp.dot(p.astype(BF16), v1, preferred_element_type=F32)
        acc_ref[g, r0:r1, :] = _lanes(alpha, hd + LANE) * acc_ref[g, r0:r1, :] + pv
        m_ref[g, r0:r1, :] = m_new

    def body(j, c):
        for g in range(heads):
            block(g, 0, tq, j, tk, False)
        return c

    lax.fori_loop(0, qi, body, 0)
    for r0 in range(0, tq, tr):
        for g in range(heads):
            block(g, r0, r0 + tr, qi, r0 + tr, True)
    for g in range(heads):
        acc = acc_ref[g]
        o_ref[0, :, g * hd:(g + 1) * hd] = (acc[:, :hd] / acc[:, hd:]).astype(o_ref.dtype)


def _fox_attention(q, kv, log_f_cum, n_heads, head_dim, out_dtype, heads_per_step):
    bsz, seq, d = q.shape
    g = heads_per_step
    assert head_dim == LANE
    tq = tk = _pick(seq, (512, 256, 128))
    tr = min(tq, 256)
    width = g * head_dim
    groups = n_heads // g
    fk = log_f_cum.reshape(bsz, n_heads, seq // tk, tk)
    kern = functools.partial(_fox_kernel, tq=tq, tk=tk, tr=tr, heads=g, hd=head_dim)
    blk = (2 * (2 * _nbytes((tq, width), BF16) + 2 * _nbytes((seq, width), BF16)
                + g * _nbytes((seq // tk, tk), F32))
           + 4 * g * _nbytes((tq, LANE), F32) + 8 * g * _nbytes((tq, tk), F32))
    return pl.pallas_call(
        kern,
        grid=(bsz, groups, seq // tq),
        in_specs=[pl.BlockSpec((1, tq, width), lambda b, h, i: (b, i, h)),
                  pl.BlockSpec((1, seq, width), lambda b, h, i: (b, 0, h)),
                  pl.BlockSpec((1, seq, width), lambda b, h, i: (b, 0, groups + h)),
                  pl.BlockSpec((1, g, seq // tk, tk), lambda b, h, i: (b, h, 0, 0))],
        out_specs=pl.BlockSpec((1, tq, width), lambda b, h, i: (b, i, h)),
        out_shape=jax.ShapeDtypeStruct((bsz, seq, d), out_dtype),
        scratch_shapes=[pltpu.VMEM((g, tq, LANE), F32), pltpu.VMEM((g, tq, LANE), F32),
                        pltpu.VMEM((g, tq, head_dim + LANE), F32)],
        compiler_params=_params(("arbitrary", "arbitrary", "arbitrary"), blk),
        name="fox_attention",
    )(q, kv, kv, fk)


def _forget_kernel(h_ref, w_ref, b_ref, o_ref, carry_ref, *, ts):
    @pl.when(pl.program_id(1) == 0)
    def _():
        carry_ref[...] = jnp.zeros_like(carry_ref)

    logit = jnp.dot(h_ref[0], w_ref[...].astype(BF16), preferred_element_type=F32) + b_ref[...]
    log_f = -_softplus(-logit) * LOG2E
    row = lax.broadcasted_iota(jnp.int32, (ts, ts), 0)
    col = lax.broadcasted_iota(jnp.int32, (ts, ts), 1)
    upto = (col <= row).astype(F32)
    csum = jnp.dot(upto, log_f, preferred_element_type=F32,
                   precision=lax.Precision.HIGHEST) + carry_ref[...]
    o_ref[0] = csum
    carry_ref[...] = csum[ts - 1:ts, :]


def _forget_cumsum(h3, w_f, b_f):
    bsz, seq, d = h3.shape
    n_heads = w_f.shape[1]
    lanes = FORGET_LANES
    w_pad = jnp.zeros((d, lanes), F32).at[:, :n_heads].set(w_f)
    b_pad = jnp.zeros((1, lanes), F32).at[0, :n_heads].set(b_f.astype(F32))
    ts = _pick(seq, (256, 128, 64, 32, 16, 8))
    blk = 2 * (_nbytes((ts, d), BF16) + _nbytes((d, lanes), F32) + _nbytes((ts, lanes), F32)) \
        + 4 * _nbytes((ts, ts), F32)
    out = pl.pallas_call(
        functools.partial(_forget_kernel, ts=ts),
        grid=(bsz, seq // ts),
        in_specs=[pl.BlockSpec((1, ts, d), lambda b, i: (b, i, 0)),
                  pl.BlockSpec((d, lanes), lambda b, i: (0, 0)),
                  pl.BlockSpec((1, lanes), lambda b, i: (0, 0))],
        out_specs=pl.BlockSpec((1, ts, lanes), lambda b, i: (b, i, 0)),
        out_shape=jax.ShapeDtypeStruct((bsz, seq, lanes), F32),
        scratch_shapes=[pltpu.VMEM((1, lanes), F32)],
        compiler_params=_params(("arbitrary", "arbitrary"), blk),
        name="forget_cumsum",
    )(h3, w_pad, b_pad)
    return out[:, :, :n_heads]


def _ln_rows(v, g, b):
    mu = jnp.mean(v, axis=-1, keepdims=True)
    cen = v - mu
    var = jnp.mean(cen * cen, axis=-1, keepdims=True)
    return cen * lax.rsqrt(var + LN_EPS) * g + b


PACKED = jnp.uint32


def _pack_halves(h):
    half = h.shape[1] // 2
    lo = lax.bitcast_convert_type(h[:, :half].astype(BF16).astype(F32), jnp.uint32)
    hi = lax.bitcast_convert_type(h[:, half:].astype(BF16).astype(F32), jnp.uint32)
    return hi | (lo >> 16)


def _unpack_halves(u):
    lo = lax.bitcast_convert_type(u << 16, F32).astype(BF16)
    hi = lax.bitcast_convert_type(u & jnp.uint32(0xFFFF0000), F32).astype(BF16)
    return jnp.concatenate([lo, hi], axis=1)


def _emit_mods(xn, mod_refs, out_refs):
    for m in range(len(out_refs)):
        sh_ref, sc_ref = mod_refs[2 * m], mod_refs[2 * m + 1]
        h = xn * sc_ref[0] + sh_ref[0]
        if out_refs[m].dtype == PACKED:
            out_refs[m][...] = _pack_halves(h)
        else:
            out_refs[m][...] = h.astype(out_refs[m].dtype)


def _split_bf16(v):
    hi = v.astype(BF16)
    return hi, (v - hi.astype(F32)).astype(BF16)


def _route_top2(h, w_ref, b_ref, idx_ref, wgt_ref, n_experts):
    h_hi, h_lo = _split_bf16(h)
    w_hi, w_lo = _split_bf16(w_ref[...])
    logits = (jnp.dot(h_hi, w_hi, preferred_element_type=F32)
              + jnp.dot(h_lo, w_hi, preferred_element_type=F32)
              + jnp.dot(h_hi, w_lo, preferred_element_type=F32)) + b_ref[...]
    lane = lax.broadcasted_iota(jnp.int32, logits.shape, 1)
    logits = jnp.where(lane < n_experts, logits, -jnp.inf)
    big = jnp.int32(ROUTER_LANES)
    v1 = jnp.max(logits, axis=1, keepdims=True)
    i1 = jnp.min(jnp.where(logits == v1, lane, big), axis=1, keepdims=True)
    rest = jnp.where(lane == i1, -jnp.inf, logits)
    v2 = jnp.max(rest, axis=1, keepdims=True)
    i2 = jnp.min(jnp.where(rest == v2, lane, big), axis=1, keepdims=True)
    e2 = jnp.exp(v2 - v1)
    w1 = 1.0 / (1.0 + e2)
    w2 = e2 / (1.0 + e2)
    idx_ref[...] = jnp.where(lane == 0, i1, jnp.where(lane == 1, i2, 0))
    wgt_ref[...] = jnp.where(lane == 0, w1, jnp.where(lane == 1, w2, 0.0))


def _ln_kernel(*refs, alpha, n_mod, n_experts):
    x_ref, y_ref, gate_ref, g_ref, b_ref = refs[:5]
    mod_refs = refs[5:5 + 2 * n_mod]
    n_in = 5 + 2 * n_mod + (2 if n_experts else 0)
    xo_ref = refs[n_in]
    h_refs = refs[n_in + 1:n_in + 1 + n_mod]
    v = alpha * x_ref[...] + gate_ref[0] * y_ref[...].astype(F32)
    xn = _ln_rows(v, g_ref[...], b_ref[...])
    xo_ref[...] = xn
    _emit_mods(xn, mod_refs, h_refs)
    if n_experts:
        wr_ref, br_ref = refs[n_in - 2:n_in]
        idx_ref, wgt_ref = refs[n_in + 1 + n_mod:]
        _route_top2(xn * mod_refs[1][0] + mod_refs[0][0], wr_ref, br_ref, idx_ref, wgt_ref,
                    n_experts)


def _residual_ln(x2, y2, gate1, ln_g, ln_b, mods, mod_dtypes, seq, alpha, router=None):
    t, d = x2.shape
    n_mod = len(mods)
    tm = _pick(seq, (256, 128, 64, 32, 16, 8))
    tpb = seq // tm
    tile = pl.BlockSpec((tm, d), lambda i: (i, 0))
    vec = pl.BlockSpec((1, 1, d), lambda i: (i // tpb, 0, 0))
    par = pl.BlockSpec((1, d), lambda i: (0, 0))
    in_specs = [tile, tile, vec, par, par] + [vec, vec] * n_mod
    args = [x2, y2, gate1, ln_g.reshape(1, d), ln_b.reshape(1, d)]
    for sh, sc1 in mods:
        args += [sh, sc1]
    def width(dt):
        return d // 2 if dt == PACKED else d

    out_specs = [tile] + [pl.BlockSpec((tm, width(dt)), lambda i: (i, 0)) for dt in mod_dtypes]
    out_shape = [jax.ShapeDtypeStruct((t, d), F32)] + \
        [jax.ShapeDtypeStruct((t, width(dt)), dt) for dt in mod_dtypes]
    n_experts = 0
    if router is not None:
        w_router, b_router = router
        n_experts = w_router.shape[1]
        lanes = ROUTER_LANES
        args += [jnp.zeros((d, lanes), F32).at[:, :n_experts].set(w_router),
                 jnp.zeros((1, lanes), F32).at[0, :n_experts].set(b_router.astype(F32))]
        in_specs += [pl.BlockSpec((d, lanes), lambda i: (0, 0)),
                     pl.BlockSpec((1, lanes), lambda i: (0, 0))]
        out_specs += [pl.BlockSpec((tm, lanes), lambda i: (i, 0))] * 2
        out_shape += [jax.ShapeDtypeStruct((t, lanes), jnp.int32),
                      jax.ShapeDtypeStruct((t, lanes), F32)]
    blk = 2 * (_nbytes((tm, d), F32) + _nbytes((tm, d), y2.dtype) + _nbytes((tm, d), F32)
               + sum(_nbytes((tm, d), dt) for dt in mod_dtypes)) + 6 * _nbytes((tm, d), F32)
    outs = pl.pallas_call(
        functools.partial(_ln_kernel, alpha=alpha, n_mod=n_mod, n_experts=n_experts),
        grid=(t // tm,),
        in_specs=in_specs,
        out_specs=out_specs,
        out_shape=out_shape,
        compiler_params=_params(("arbitrary",), blk),
        name="residual_ln",
    )(*args)
    hs = list(outs[1:1 + n_mod])
    if router is not None:
        return outs[0], hs, outs[-2][:, :TOP_K], outs[-1][:, :TOP_K]
    return outs[0], hs


def _route_plan(top_idx, n_experts, tile):
    t = top_idx.shape[0]
    p = TOP_K * t
    e_flat = top_idx.reshape(p)
    onehot = (e_flat[:, None] == jnp.arange(n_experts, dtype=jnp.int32)[None, :]).astype(jnp.int32)
    csum = jnp.cumsum(onehot, axis=0)
    counts = csum[-1]
    rank = jnp.sum(csum * onehot, axis=1) - 1
    padded = ((counts + tile - 1) // tile) * tile
    gend = jnp.cumsum(padded)
    gstart = gend - padded
    pos = jnp.sum(onehot * gstart[None, :], axis=1) + rank
    n_rows = ((p + tile - 1) // tile + n_experts) * tile
    row_token = jnp.zeros((n_rows,), jnp.int32).at[pos].set(
        jnp.arange(p, dtype=jnp.int32) // TOP_K)
    tile_start = jnp.arange(n_rows // tile, dtype=jnp.int32) * tile
    tile_expert = jnp.sum((tile_start[:, None] >= gend[None, :]).astype(jnp.int32), axis=1)
    tile_valid = (tile_start < gend[-1]).astype(jnp.int32)
    tile_expert = jnp.minimum(tile_expert, n_experts - 1)
    return pos.astype(jnp.int32), row_token, gend[-1:].astype(jnp.int32), tile_expert, tile_valid


def _row_copy(src_ref, src_row, dst_ref, dst_row, sem):
    return pltpu.make_async_copy(src_ref.at[pl.ds(src_row, 1)], dst_ref.at[pl.ds(dst_row, 1)], sem)


GATHER_UNROLL = 8
DMA_PRIORITIES = 2


def _prefetched_gather(issue, n_steps):
    i = pl.program_id(0)
    slot = i % 2

    @pl.when(i == 0)
    def _():
        issue(i, slot)

    @pl.when(i + 1 < n_steps)
    def _():
        issue(i + 1, 1 - slot)

    return slot


def _dispatch_kernel(tok_ref, total_ref, src_ref, o_ref, buf_ref, sems, *, rows, n_steps):
    def issue(step, slot):
        base = step * rows

        @pl.when(base < total_ref[0])
        def _():
            def body(r2, c):
                for k in range(DMA_PRIORITIES):
                    r = r2 * DMA_PRIORITIES + k
                    _row_copy(src_ref, tok_ref[base + r], buf_ref.at[slot], r,
                              sems.at[slot]).start(priority=k)
                return c

            lax.fori_loop(0, rows // DMA_PRIORITIES, body, 0,
                          unroll=min(GATHER_UNROLL // DMA_PRIORITIES, rows // DMA_PRIORITIES))

    slot = _prefetched_gather(issue, n_steps)
    used = pl.program_id(0) * rows < total_ref[0]

    @pl.when(used)
    def _():
        def drain(r, c):
            _row_copy(src_ref, 0, buf_ref.at[slot], r, sems.at[slot]).wait()
            return c

        lax.fori_loop(0, rows, drain, 0, unroll=min(GATHER_UNROLL, rows))
        o_ref[...] = _unpack_halves(buf_ref[slot])

    @pl.when(jnp.logical_not(used))
    def _():
        o_ref[...] = jnp.zeros_like(o_ref)


def _dispatch(h_packed, row_token, total_rows, rows):
    t, half = h_packed.shape
    d = 2 * half
    n_rows = row_token.shape[0]
    n_steps = n_rows // rows
    blk = 2 * _nbytes((rows, d), BF16) + 4 * _nbytes((rows, half), PACKED)
    grid_spec = pltpu.PrefetchScalarGridSpec(
        num_scalar_prefetch=2,
        grid=(n_steps,),
        in_specs=[pl.BlockSpec(memory_space=pl.ANY)],
        out_specs=pl.BlockSpec((rows, d), lambda i, tok, total: (i, 0)),
        scratch_shapes=[pltpu.VMEM((2, rows, half), PACKED), pltpu.SemaphoreType.DMA((2,))],
    )
    return pl.pallas_call(
        functools.partial(_dispatch_kernel, rows=rows, n_steps=n_steps),
        grid_spec=grid_spec,
        out_shape=jax.ShapeDtypeStruct((n_rows, d), BF16),
        compiler_params=_params(("arbitrary",), blk),
        name="expert_dispatch",
    )(row_token, total_rows, h_packed)


def _unpack_tiles_f32(u, pack_tile):
    half = pack_tile // 2
    pieces = []
    for c in range(0, u.shape[1], half):
        chunk = u[:, c:c + half]
        pieces.append(lax.bitcast_convert_type(chunk << 16, F32))
        pieces.append(lax.bitcast_convert_type(chunk & jnp.uint32(0xFFFF0000), F32))
    return jnp.concatenate(pieces, axis=1)


def _combine_ln_kernel(*refs, alpha, n_mod, tm, n_steps, pack_tile):
    pos_ref, x_ref, rows_ref, wgt_ref, gate_ref, g_ref, b_ref = refs[:7]
    mod_refs = refs[7:7 + 2 * n_mod]
    xo_ref = refs[7 + 2 * n_mod]
    h_refs = refs[8 + 2 * n_mod:8 + 3 * n_mod]
    buf_ref, sems = refs[8 + 3 * n_mod:]

    def issue(step, slot):
        base = step * tm

        def body(r, c):
            for k in range(TOP_K):
                _row_copy(rows_ref, pos_ref[(base + r) * TOP_K + k], buf_ref.at[slot, k], r,
                          sems.at[slot]).start(priority=k % DMA_PRIORITIES)
            return c

        lax.fori_loop(0, tm, body, 0, unroll=min(GATHER_UNROLL, tm))

    slot = _prefetched_gather(issue, n_steps)

    def drain(r, c):
        for k in range(TOP_K):
            _row_copy(rows_ref, 0, buf_ref.at[slot, k], r, sems.at[slot]).wait()
        return c

    lax.fori_loop(0, tm, drain, 0, unroll=min(GATHER_UNROLL, tm))
    wgt = wgt_ref[...]
    y = wgt[:, 0:1] * _unpack_tiles_f32(buf_ref[slot, 0], pack_tile)
    for k in range(1, TOP_K):
        y = y + wgt[:, k:k + 1] * _unpack_tiles_f32(buf_ref[slot, k], pack_tile)
    v = alpha * x_ref[...] + gate_ref[0] * y
    xn = _ln_rows(v, g_ref[...], b_ref[...])
    xo_ref[...] = xn
    _emit_mods(xn, mod_refs, h_refs)


def _combine_ln(x2, expert_rows, pack_tile, pos, top_w, gate1, ln_g, ln_b, mods, mod_dtypes, seq,
                alpha):
    t, d = x2.shape
    n_mod = len(mods)
    tm = _pick(seq, (128, 64, 32, 16, 8))
    tpb = seq // tm
    n_steps = t // tm
    tile = pl.BlockSpec((tm, d), lambda i, pos: (i, 0))
    vec = pl.BlockSpec((1, 1, d), lambda i, pos: (i // tpb, 0, 0))
    par = pl.BlockSpec((1, d), lambda i, pos: (0, 0))
    wgt = pl.BlockSpec((tm, TOP_K), lambda i, pos: (i, 0))
    in_specs = [tile, pl.BlockSpec(memory_space=pl.ANY), wgt, vec, par, par] + [vec, vec] * n_mod
    args = [x2, expert_rows, top_w, gate1, ln_g.reshape(1, d), ln_b.reshape(1, d)]
    for sh, sc1 in mods:
        args += [sh, sc1]
    out_shape = [jax.ShapeDtypeStruct((t, d), F32)] + \
        [jax.ShapeDtypeStruct((t, d), dt) for dt in mod_dtypes]
    blk = 2 * (2 * _nbytes((tm, d), F32) + sum(_nbytes((tm, d), dt) for dt in mod_dtypes)) \
        + (TOP_K + 6) * _nbytes((tm, d), F32)
    grid_spec = pltpu.PrefetchScalarGridSpec(
        num_scalar_prefetch=1,
        grid=(n_steps,),
        in_specs=in_specs,
        out_specs=[tile] * (1 + n_mod),
        scratch_shapes=[pltpu.VMEM((2, TOP_K, tm, d // 2), PACKED),
                        pltpu.SemaphoreType.DMA((2,))],
    )
    outs = pl.pallas_call(
        functools.partial(_combine_ln_kernel, alpha=alpha, n_mod=n_mod, tm=tm, n_steps=n_steps,
                          pack_tile=pack_tile),
        grid_spec=grid_spec,
        out_shape=out_shape,
        compiler_params=_params(("arbitrary",), blk),
        name="combine_ln",
    )(pos, *args)
    return outs[0], list(outs[1:])


def kernel(x, c, ada_w, ada_b, ada_table, kv_table, a_w_qkv, a_w_o, kv_w, kv_b_f, b_w_q, b_w_o,
           ln_g, ln_b, ffn_w_up, ffn_w_down, moe_w_router, moe_b_router, moe_w_up, moe_w_down):
    bsz, seq, d = x.shape
    t = bsz * seq
    depth, n_mod = ada_table.shape[0], ada_table.shape[1]
    n_a = a_w_qkv.shape[0]
    n_heads = kv_w.shape[1] - 2 * d
    head_dim = d // n_heads
    n_experts = moe_w_router.shape[-1]
    alpha = (2.0 * depth) ** 0.25
    expert_tile = _pick(TOP_K * t, (512, 256, 128, 64, 32, 16))
    q_scale = head_dim ** -0.5 * LOG2E
    stick_heads = STICK_HEADS_PER_STEP if n_heads % STICK_HEADS_PER_STEP == 0 else 1
    fox_heads = FOX_HEADS_PER_STEP if n_heads % FOX_HEADS_PER_STEP == 0 else 1

    mod = _ada_mod(c, ada_w, ada_b).reshape(bsz, n_mod, d)

    def vec3(v):
        return v.reshape(bsz, 1, d)

    def layer_mods(l):
        m = mod + ada_table[l][None]
        return [m[:, i] for i in range(n_mod)]

    def mix_mod(l):
        m = layer_mods(l)
        return vec3(m[0]), vec3(1.0 + m[1])

    kv_mod = (vec3(mod[:, 0] + kv_table[0]), vec3(1.0 + mod[:, 1] + kv_table[1]))

    x2 = x.reshape(t, d)
    h = _modcast(x2, *mix_mod(0), seq, BF16)
    h_kv = None
    shared = None
    for l in range(depth):
        m = layer_mods(l)
        gate_mix1, gate_ffn1 = vec3(1.0 + m[2]), vec3(1.0 + m[5])
        ffn_mod = (vec3(m[3]), vec3(1.0 + m[4]))
        is_moe = l % 2 == 1

        if l < n_a:
            qkv = _matmul(h, a_w_qkv, l, 0, 3 * d, BF16, scaled_cols=d, col_scale=-q_scale)
            o = _stick_attention(qkv.reshape(bsz, seq, 3 * d), n_heads, head_dim, BF16,
                                 stick_heads)
            mix = _matmul(o.reshape(t, d), a_w_o, l, 0, d, BRANCH_DTYPE)
        else:
            if shared is None:
                kv = _matmul(h_kv, kv_w.T, None, 0, 2 * d, BF16, transposed=True)
                log_f = _forget_cumsum(h_kv.reshape(bsz, seq, d), kv_w[:, 2 * d:], kv_b_f)
                shared = (kv.reshape(bsz, seq, 2 * d), jnp.transpose(log_f, (0, 2, 1)))
            j = l - n_a
            q = _matmul(h, b_w_q, j, 0, d, BF16, scaled_cols=d, col_scale=q_scale)
            o = _fox_attention(q.reshape(bsz, seq, d), shared[0], shared[1],
                               n_heads, head_dim, BF16, fox_heads)
            mix = _matmul(o.reshape(t, d), b_w_o, j, 0, d, BRANCH_DTYPE)
        if is_moe:
            x2, (h_ffn,), top_idx, top_w = _residual_ln(
                x2, mix, gate_mix1, ln_g[l, 0], ln_b[l, 0], [ffn_mod], [PACKED], seq, alpha,
                router=(moe_w_router[l // 2], moe_b_router[l // 2]))
        else:
            x2, (h_ffn,) = _residual_ln(x2, mix, gate_mix1, ln_g[l, 0], ln_b[l, 0], [ffn_mod],
                                        [BF16], seq, alpha)

        next_mods, next_dtypes = [], []
        if l + 1 < depth:
            next_mods.append(mix_mod(l + 1))
            next_dtypes.append(BF16)
            if l + 1 == n_a:
                next_mods.append(kv_mod)
                next_dtypes.append(BF16)
        if not is_moe:
            act = _swiglu_up(h_ffn, ffn_w_up, l // 2)
            y = _matmul(act, ffn_w_down, l // 2, 0, d, BRANCH_DTYPE)
            x2, hs = _residual_ln(x2, y, gate_ffn1, ln_g[l, 1], ln_b[l, 1], next_mods,
                                  next_dtypes, seq, alpha)
        else:
            e = l // 2
            pos, row_token, total_rows, tile_expert, tile_valid = _route_plan(
                top_idx, n_experts, expert_tile)
            hg = _dispatch(h_ffn, row_token, total_rows, min(expert_tile, 256))
            next_group = _next_group_tile(tile_expert)
            act = _grouped_swiglu_up(hg, moe_w_up, e, tile_expert, tile_valid, next_group,
                                     expert_tile)
            rows, pack_tile = _grouped_matmul(act, moe_w_down, e, tile_expert, tile_valid,
                                              next_group, expert_tile)
            x2, hs = _combine_ln(x2, rows, pack_tile, pos, top_w, gate_ffn1, ln_g[l, 1],
                                 ln_b[l, 1], next_mods, next_dtypes, seq, alpha)
        if hs:
            h = hs[0]
            if len(hs) > 1:
                h_kv = hs[1]
    return x2.reshape(bsz, seq, d)
```

```python
import functools

import jax
import jax.numpy as jnp
from jax import lax
from jax.experimental import pallas as pl
from jax.experimental.pallas import tpu as pltpu

F32 = jnp.float32
BF16 = jnp.bfloat16

LN_EPS = 1e-5
TOP_K = 2
NEG_INF = -1e30
LOG2E = 1.4426950408889634
LANE = 128
FORGET_LANES = LANE
ROUTER_LANES = LANE
BRANCH_DTYPE = BF16
STICK_HEADS_PER_STEP = 4
FOX_HEADS_PER_STEP = 8

V7X_VMEM_BYTES = 64 * 1024 * 1024
VMEM_HEADROOM_BYTES = 12 * 1024 * 1024
VMEM_CAP_BYTES = V7X_VMEM_BYTES - 6 * 1024 * 1024


def _params(semantics, block_bytes):
    limit = min(int(block_bytes) + VMEM_HEADROOM_BYTES, VMEM_CAP_BYTES)
    return pltpu.CompilerParams(dimension_semantics=semantics, vmem_limit_bytes=limit)


def _nbytes(shape, dtype):
    n = 1
    for s in shape:
        n *= s
    return n * jnp.dtype(dtype).itemsize


def _pick(n, prefs):
    for p in prefs:
        if n % p == 0:
            return p
    return n


def _ada_kernel(c_ref, w_ref, b_ref, o_ref):
    c = c_ref[...]
    s = c * jax.nn.sigmoid(c)
    o_ref[...] = jnp.dot(s.astype(BF16), w_ref[...].astype(BF16),
                         preferred_element_type=F32) + b_ref[...]


def _ada_mod(c, ada_w, ada_b):
    bsz, d = c.shape
    n = ada_w.shape[1]
    rows = 8
    c8 = jnp.zeros((rows, d), F32).at[:bsz].set(c)
    tn = _pick(n, (512, 256, 128))
    blk = 2 * (_nbytes((rows, d), F32) + _nbytes((d, tn), F32) + 2 * _nbytes((rows, tn), F32))
    out = pl.pallas_call(
        _ada_kernel,
        grid=(n // tn,),
        in_specs=[pl.BlockSpec((rows, d), lambda j: (0, 0)),
                  pl.BlockSpec((d, tn), lambda j: (0, j)),
                  pl.BlockSpec((1, tn), lambda j: (0, j))],
        out_specs=pl.BlockSpec((rows, tn), lambda j: (0, j)),
        out_shape=jax.ShapeDtypeStruct((rows, n), F32),
        compiler_params=_params(("arbitrary",), blk),
        name="ada_mod",
    )(c8, ada_w, ada_b.reshape(1, n))
    return out[:bsz]


def _modcast_kernel(x_ref, sh_ref, sc_ref, o_ref):
    o_ref[...] = (x_ref[...] * sc_ref[0] + sh_ref[0]).astype(o_ref.dtype)


def _modcast(x2, shift, scale1, seq, out_dtype):
    t, d = x2.shape
    tm = _pick(seq, (256, 128, 64, 32, 16, 8))
    tpb = seq // tm
    vec = pl.BlockSpec((1, 1, d), lambda i: (i // tpb, 0, 0))
    blk = 2 * (_nbytes((tm, d), F32) + _nbytes((tm, d), out_dtype))
    return pl.pallas_call(
        _modcast_kernel,
        grid=(t // tm,),
        in_specs=[pl.BlockSpec((tm, d), lambda i: (i, 0)), vec, vec],
        out_specs=pl.BlockSpec((tm, d), lambda i: (i, 0)),
        out_shape=jax.ShapeDtypeStruct((t, d), out_dtype),
        compiler_params=_params(("arbitrary",), blk),
        name="modcast",
    )(x2, shift, scale1)


def _weight_columns(w_hbm, w_index, col, width, transposed=False):
    src = w_hbm if w_index is None else w_hbm.at[w_index]
    window = pl.ds(pl.multiple_of(col, width), width)
    return src.at[window] if transposed else src.at[:, window]


def _stage_weights(w_hbm, w_index, cols, stage_ref, wb_ref, sems, width, transposed=False):
    j = pl.program_id(0)
    n_steps = pl.num_programs(0)

    def copies(jj):
        return [pltpu.make_async_copy(_weight_columns(w_hbm, w_index, c, width, transposed),
                                      stage_ref.at[s], sems.at[s])
                for s, c in enumerate(cols(jj))]

    @pl.when(pl.program_id(1) == 0)
    def _():
        @pl.when(j == 0)
        def _():
            for cp in copies(j):
                cp.start()

        for s, cp in enumerate(copies(j)):
            cp.wait()
            wb_ref[s] = stage_ref[s].astype(BF16)

        @pl.when(j + 1 < n_steps)
        def _():
            for cp in copies(j + 1):
                cp.start()


def _mm_kernel(a_ref, w_hbm, o_ref, stage_ref, wb_ref, sems,
               *, w_index, col0, tn, scaled_blocks, col_scale, transposed):
    _stage_weights(w_hbm, w_index, lambda jj: [col0 + jj * tn], stage_ref, wb_ref, sems, tn,
                   transposed)
    contract_w = 1 if transposed else 0
    res = lax.dot_general(a_ref[...], wb_ref[0], (((1,), (contract_w,)), ((), ())),
                          preferred_element_type=F32)
    if scaled_blocks:
        res = res * jnp.where(pl.program_id(0) < scaled_blocks, col_scale, 1.0)
    o_ref[...] = res.astype(o_ref.dtype)


MM_BLOCK_ELEMS = 4 * 1024 * 1024


def _mm_tiles(m, k, n):
    tn = _pick(n, tuple(c for c in (1024, 512, 256, 128) if c * k <= MM_BLOCK_ELEMS))
    tm = _pick(m, tuple(t for t in (1024, 512, 256, 128, 64, 32, 16, 8) if t * k <= MM_BLOCK_ELEMS))
    return tm, tn


def _matmul(a, w, w_index, col0, n, out_dtype, scaled_cols=0, col_scale=1.0, transposed=False):
    m, k = a.shape
    tm, tn = _mm_tiles(m, k, n)
    assert scaled_cols % tn == 0 and col0 % tn == 0
    blk = (2 * (_nbytes((tm, k), BF16) + _nbytes((tm, tn), out_dtype))
           + _nbytes((k, tn), F32) + _nbytes((k, tn), BF16) + _nbytes((tm, tn), F32))
    kern = functools.partial(_mm_kernel, w_index=w_index, col0=col0, tn=tn,
                             scaled_blocks=scaled_cols // tn, col_scale=col_scale,
                             transposed=transposed)
    w_block = (tn, k) if transposed else (k, tn)
    return pl.pallas_call(
        kern,
        grid=(n // tn, m // tm),
        in_specs=[pl.BlockSpec((tm, k), lambda j, i: (i, 0)),
                  pl.BlockSpec(memory_space=pl.ANY)],
        out_specs=pl.BlockSpec((tm, tn), lambda j, i: (i, j)),
        out_shape=jax.ShapeDtypeStruct((m, n), out_dtype),
        scratch_shapes=[pltpu.VMEM((1,) + w_block, F32), pltpu.VMEM((1,) + w_block, BF16),
                        pltpu.SemaphoreType.DMA((1,))],
        compiler_params=_params(("arbitrary", "arbitrary"), blk),
        name="matmul",
    )(a, w)


def _swiglu_kernel(a_ref, w_hbm, o_ref, stage_ref, wb_ref, sems, *, w_index, tn, f):
    _stage_weights(w_hbm, w_index, lambda jj: [jj * tn, f + jj * tn], stage_ref, wb_ref, sems, tn)
    a = a_ref[...]
    g = jnp.dot(a, wb_ref[0], preferred_element_type=F32)
    u = jnp.dot(a, wb_ref[1], preferred_element_type=F32)
    o_ref[...] = (g * jax.nn.sigmoid(g) * u).astype(o_ref.dtype)


def _swiglu_up(a, w_up, w_index):
    m, k = a.shape
    f = w_up.shape[-1] // 2
    tn = _pick(f, tuple(c for c in (512, 256, 128) if 2 * c * k <= MM_BLOCK_ELEMS))
    tm = _pick(m, tuple(t for t in (1024, 512, 256, 128, 64, 32, 16, 8) if t * k <= MM_BLOCK_ELEMS))
    blk = (2 * (_nbytes((tm, k), BF16) + _nbytes((tm, tn), BF16))
           + 2 * (_nbytes((k, tn), F32) + _nbytes((k, tn), BF16)) + 3 * _nbytes((tm, tn), F32))
    return pl.pallas_call(
        functools.partial(_swiglu_kernel, w_index=w_index, tn=tn, f=f),
        grid=(f // tn, m // tm),
        in_specs=[pl.BlockSpec((tm, k), lambda j, i: (i, 0)),
                  pl.BlockSpec(memory_space=pl.ANY)],
        out_specs=pl.BlockSpec((tm, tn), lambda j, i: (i, j)),
        out_shape=jax.ShapeDtypeStruct((m, f), BF16),
        scratch_shapes=[pltpu.VMEM((2, k, tn), F32), pltpu.VMEM((2, k, tn), BF16),
                        pltpu.SemaphoreType.DMA((2,))],
        compiler_params=_params(("arbitrary", "arbitrary"), blk),
        name="swiglu_up",
    )(a, w_up)


def _expert_changed(te_ref, i):
    prev = te_ref[jnp.maximum(i - 1, 0)]
    return jnp.logical_or(i == 0, te_ref[i] != prev)


def _next_group_tile(tile_expert):
    n = tile_expert.shape[0]
    idx = jnp.arange(n, dtype=jnp.int32)
    first = jnp.concatenate([jnp.ones((1,), bool), tile_expert[1:] != tile_expert[:-1]])
    cand = jnp.where(first, idx, n)
    from_here = lax.cummin(cand, axis=0, reverse=True)
    return jnp.concatenate([from_here[1:], jnp.full((1,), n, jnp.int32)])


def _stage_expert_weights(w_hbm, w_index, te_ref, nxt_ref, cols, stage_ref, wb_ref, sems, width):
    j, i = pl.program_id(0), pl.program_id(1)
    n_j, n_i = pl.num_programs(0), pl.num_programs(1)

    def copies(expert, jj):
        src = w_hbm.at[w_index, expert]
        return [pltpu.make_async_copy(src.at[:, pl.ds(pl.multiple_of(c, width), width)],
                                      stage_ref.at[s], sems.at[s])
                for s, c in enumerate(cols(jj))]

    @pl.when(_expert_changed(te_ref, i))
    def _():
        @pl.when(jnp.logical_and(j == 0, i == 0))
        def _():
            for cp in copies(te_ref[0], j):
                cp.start()

        for s, cp in enumerate(copies(te_ref[i], j)):
            cp.wait()
            wb_ref[s] = stage_ref[s].astype(BF16)

        nxt = nxt_ref[i]

        @pl.when(nxt < n_i)
        def _():
            for cp in copies(te_ref[nxt], j):
                cp.start()

        @pl.when(jnp.logical_and(nxt >= n_i, j + 1 < n_j))
        def _():
            for cp in copies(te_ref[0], j + 1):
                cp.start()


def _gswiglu_kernel(te_ref, tv_ref, nxt_ref, a_ref, w_hbm, o_ref, stage_ref, wb_ref, sems,
                    *, w_index, tn, f):
    i = pl.program_id(1)
    _stage_expert_weights(w_hbm, w_index, te_ref, nxt_ref, lambda jj: [jj * tn, f + jj * tn],
                          stage_ref, wb_ref, sems, tn)

    @pl.when(tv_ref[i] > 0)
    def _():
        a = a_ref[...]
        g = jnp.dot(a, wb_ref[0], preferred_element_type=F32)
        u = jnp.dot(a, wb_ref[1], preferred_element_type=F32)
        o_ref[...] = (g * jax.nn.sigmoid(g) * u).astype(o_ref.dtype)

    @pl.when(tv_ref[i] == 0)
    def _():
        o_ref[...] = jnp.zeros_like(o_ref)


def _grouped_swiglu_up(a, w_up, w_index, tile_expert, tile_valid, next_group, tile):
    p, k = a.shape
    f = w_up.shape[-1] // 2
    tn = _pick(f, tuple(c for c in (512, 256, 128) if 2 * c * k <= MM_BLOCK_ELEMS))
    blk = (2 * (_nbytes((tile, k), BF16) + _nbytes((tile, tn), BF16))
           + 2 * (_nbytes((k, tn), F32) + _nbytes((k, tn), BF16)) + 3 * _nbytes((tile, tn), F32))
    grid_spec = pltpu.PrefetchScalarGridSpec(
        num_scalar_prefetch=3,
        grid=(f // tn, p // tile),
        in_specs=[pl.BlockSpec((tile, k), lambda j, i, te, tv, nx: (i, 0)),
                  pl.BlockSpec(memory_space=pl.ANY)],
        out_specs=pl.BlockSpec((tile, tn), lambda j, i, te, tv, nx: (i, j)),
        scratch_shapes=[pltpu.VMEM((2, k, tn), F32), pltpu.VMEM((2, k, tn), BF16),
                        pltpu.SemaphoreType.DMA((2,))],
    )
    return pl.pallas_call(
        functools.partial(_gswiglu_kernel, w_index=w_index, tn=tn, f=f),
        grid_spec=grid_spec,
        out_shape=jax.ShapeDtypeStruct((p, f), BF16),
        compiler_params=_params(("arbitrary", "arbitrary"), blk),
        name="expert_swiglu_up",
    )(tile_expert, tile_valid, next_group, a, w_up)


def _gmm_kernel(te_ref, tv_ref, nxt_ref, a_ref, w_hbm, o_ref, stage_ref, wb_ref, sems,
                *, w_index, tn):
    i = pl.program_id(1)
    _stage_expert_weights(w_hbm, w_index, te_ref, nxt_ref, lambda jj: [jj * tn],
                          stage_ref, wb_ref, sems, tn)

    @pl.when(tv_ref[i] > 0)
    def _():
        o_ref[...] = jnp.dot(a_ref[...], wb_ref[0],
                             preferred_element_type=F32).astype(o_ref.dtype)

    @pl.when(tv_ref[i] == 0)
    def _():
        o_ref[...] = jnp.zeros_like(o_ref)


def _grouped_matmul(a, w, w_index, tile_expert, tile_valid, next_group, tile, out_dtype):
    p, k = a.shape
    n = w.shape[-1]
    tn = _pick(n, tuple(c for c in (2048, 1024, 512, 256, 128) if k * c <= MM_BLOCK_ELEMS))
    blk = (2 * (_nbytes((tile, k), BF16) + _nbytes((tile, tn), out_dtype))
           + _nbytes((k, tn), F32) + _nbytes((k, tn), BF16) + _nbytes((tile, tn), F32))
    grid_spec = pltpu.PrefetchScalarGridSpec(
        num_scalar_prefetch=3,
        grid=(n // tn, p // tile),
        in_specs=[pl.BlockSpec((tile, k), lambda j, i, te, tv, nx: (i, 0)),
                  pl.BlockSpec(memory_space=pl.ANY)],
        out_specs=pl.BlockSpec((tile, tn), lambda j, i, te, tv, nx: (i, j)),
        scratch_shapes=[pltpu.VMEM((1, k, tn), F32), pltpu.VMEM((1, k, tn), BF16),
                        pltpu.SemaphoreType.DMA((1,))],
    )
    return pl.pallas_call(
        functools.partial(_gmm_kernel, w_index=w_index, tn=tn),
        grid_spec=grid_spec,
        out_shape=jax.ShapeDtypeStruct((p, n), out_dtype),
        compiler_params=_params(("arbitrary", "arbitrary"), blk),
        name="expert_matmul",
    )(tile_expert, tile_valid, next_group, a, w)


def _softplus(z):
    return jnp.maximum(z, 0.0) + jnp.log(1.0 + jnp.exp(-jnp.abs(z)))


def _neg_abs(x):
    bits = lax.bitcast_convert_type(x, jnp.uint32) | jnp.uint32(0x80000000)
    return lax.bitcast_convert_type(bits, F32)


def _lanes(x, n):
    reps = n // x.shape[1]
    return x if reps == 1 else jnp.concatenate([x] * reps, axis=1)


def _stick_kernel(q_ref, k_ref, v_ref, o_ref, acc_ref, carry_ref, *, tq, tk, heads, hd):
    qi = pl.program_id(2)
    n_sub = tq // tk
    row = lax.broadcasted_iota(jnp.int32, (tk, tk), 0)
    col = lax.broadcasted_iota(jnp.int32, (tk, tk), 1)
    later = (row > col).astype(BF16)
    acc_ref[...] = jnp.zeros_like(acc_ref)
    carry_ref[...] = jnp.zeros_like(carry_ref)

    def block(g, r0, k0, diag_off):
        nr = tq - r0
        lanes = slice(g * hd, (g + 1) * hd)
        q = q_ref[0, r0:tq, lanes]
        k = k_ref[0, pl.ds(k0, tk), lanes]
        v = v_ref[0, pl.ds(k0, tk), lanes]
        s = lax.dot_general(q, k, (((1,), (1,)), ((), ())), preferred_element_type=F32)
        soft = jnp.log(1.0 + jnp.exp2(_neg_abs(s))) * LOG2E
        log_keep = jnp.minimum(s, 0.0) - soft
        if diag_off is not None:
            q_pos = r0 + lax.broadcasted_iota(jnp.int32, (nr, tk), 0)
            k_pos = diag_off + lax.broadcasted_iota(jnp.int32, (nr, tk), 1)
            mask = k_pos < q_pos
            log_keep = jnp.where(mask, log_keep, 0.0)
        suffix = jnp.dot(log_keep.astype(BF16), later, preferred_element_type=F32)
        carry = carry_ref[g, r0:tq, :]
        w = jnp.exp2((log_keep - s) + suffix + _lanes(carry, tk))
        if diag_off is not None:
            w = jnp.where(mask, w, 0.0)
        acc_ref[g, r0:tq, :] += jnp.dot(w.astype(BF16), v, preferred_element_type=F32)
        carry_ref[g, r0:tq, :] = carry + jnp.sum(log_keep, axis=1, keepdims=True)

    diag0 = qi * tq
    for sub in range(n_sub - 1, -1, -1):
        for g in range(heads):
            block(g, sub * tk, pl.multiple_of(diag0 + sub * tk, tk), sub * tk)

    def body(j, c):
        k0 = pl.multiple_of(diag0 - (j + 1) * tk, tk)
        for g in range(heads):
            block(g, 0, k0, None)
        return c

    lax.fori_loop(0, qi * n_sub, body, 0)
    for g in range(heads):
        o_ref[0, :, g * hd:(g + 1) * hd] = acc_ref[g].astype(o_ref.dtype)


def _attn_tiles(seq):
    tq = _pick(seq, (1024, 512, 256, 128))
    tk = min(tq, 256)
    return tq, tk


def _stick_attention(qkv, n_heads, head_dim, out_dtype, heads_per_step):
    bsz, seq, _ = qkv.shape
    d = n_heads * head_dim
    g = heads_per_step
    tq, tk = _attn_tiles(seq)
    width = g * head_dim
    kern = functools.partial(_stick_kernel, tq=tq, tk=tk, heads=g, hd=head_dim)
    blk = (2 * (2 * _nbytes((tq, width), BF16) + 2 * _nbytes((seq, width), BF16))
           + g * (_nbytes((tq, head_dim), F32) + _nbytes((tq, LANE), F32))
           + 8 * g * _nbytes((tq, tk), F32))
    groups = n_heads // g
    return pl.pallas_call(
        kern,
        grid=(bsz, groups, seq // tq),
        in_specs=[pl.BlockSpec((1, tq, width), lambda b, h, i: (b, i, h)),
                  pl.BlockSpec((1, seq, width), lambda b, h, i: (b, 0, groups + h)),
                  pl.BlockSpec((1, seq, width), lambda b, h, i: (b, 0, 2 * groups + h))],
        out_specs=pl.BlockSpec((1, tq, width), lambda b, h, i: (b, i, h)),
        out_shape=jax.ShapeDtypeStruct((bsz, seq, d), out_dtype),
        scratch_shapes=[pltpu.VMEM((g, tq, head_dim), F32), pltpu.VMEM((g, tq, LANE), F32)],
        compiler_params=_params(("arbitrary", "arbitrary", "arbitrary"), blk),
        name="stick_attention",
    )(qkv, qkv, qkv)


def _fox_kernel(q_ref, k_ref, v_ref, fk_ref, o_ref, m_ref, fqb_ref, acc_ref,
                *, tq, tk, tr, heads, hd):
    qi = pl.program_id(2)
    assert tq == tk
    m_ref[...] = jnp.full_like(m_ref, NEG_INF)
    acc_ref[...] = jnp.zeros_like(acc_ref)
    for g in range(heads):
        f_row = fk_ref[0, g, pl.ds(qi, 1), :]
        for c in range(tk // LANE):
            chunk = f_row[:, c * LANE:(c + 1) * LANE]
            fqb_ref[g, c * LANE:(c + 1) * LANE, :] = jnp.broadcast_to(chunk, (LANE, LANE)).T

    def block(g, r0, r1, kb, nk, masked):
        nr = r1 - r0
        lanes = slice(g * hd, (g + 1) * hd)
        k0 = pl.multiple_of(kb * tk, tk)
        q = q_ref[0, r0:r1, lanes]
        k = k_ref[0, pl.ds(k0, nk), lanes]
        v1 = jnp.concatenate([v_ref[0, pl.ds(k0, nk), lanes], jnp.ones((nk, LANE), BF16)], axis=1)
        fk = fk_ref[0, g, pl.ds(kb, 1), :][:, :nk]
        u = lax.dot_general(q, k, (((1,), (1,)), ((), ())), preferred_element_type=F32) - fk
        if masked:
            q_pos = r0 + lax.broadcasted_iota(jnp.int32, (nr, nk), 0)
            k_pos = lax.broadcasted_iota(jnp.int32, (nr, nk), 1)
            u = jnp.where(k_pos <= q_pos, u, NEG_INF)
        fq = fqb_ref[g, r0:r1, :]
        m_prev = m_ref[g, r0:r1, :]
        m_new = jnp.maximum(m_prev, jnp.max(u, axis=1, keepdims=True) + fq)
        alpha = jnp.exp2(m_prev - m_new)
        p = jnp.exp2(u + _lanes(fq - m_new, nk))
        pv = jnp.dot(p.astype(BF16), v1, preferred_element_type=F32)
        acc_ref[g, r0:r1, :] = _lanes(alpha, hd + LANE) * acc_ref[g, r0:r1, :] + pv
        m_ref[g, r0:r1, :] = m_new

    def body(j, c):
        for g in range(heads):
            block(g, 0, tq, j, tk, False)
        return c

    lax.fori_loop(0, qi, body, 0)
    for r0 in range(0, tq, tr):
        for g in range(heads):
            block(g, r0, r0 + tr, qi, r0 + tr, True)
    for g in range(heads):
        acc = acc_ref[g]
        o_ref[0, :, g * hd:(g + 1) * hd] = (acc[:, :hd] / acc[:, hd:]).astype(o_ref.dtype)


def _fox_attention(q, kv, log_f_cum, n_heads, head_dim, out_dtype, heads_per_step):
    bsz, seq, d = q.shape
    g = heads_per_step
    assert head_dim == LANE
    tq = tk = _pick(seq, (512, 256, 128))
    tr = min(tq, 256)
    width = g * head_dim
    groups = n_heads // g
    fk = log_f_cum.reshape(bsz, n_heads, seq // tk, tk)
    kern = functools.partial(_fox_kernel, tq=tq, tk=tk, tr=tr, heads=g, hd=head_dim)
    blk = (2 * (2 * _nbytes((tq, width), BF16) + 2 * _nbytes((seq, width), BF16)
                + g * _nbytes((seq // tk, tk), F32))
           + 4 * g * _nbytes((tq, LANE), F32) + 8 * g * _nbytes((tq, tk), F32))
    return pl.pallas_call(
        kern,
        grid=(bsz, groups, seq // tq),
        in_specs=[pl.BlockSpec((1, tq, width), lambda b, h, i: (b, i, h)),
                  pl.BlockSpec((1, seq, width), lambda b, h, i: (b, 0, h)),
                  pl.BlockSpec((1, seq, width), lambda b, h, i: (b, 0, groups + h)),
                  pl.BlockSpec((1, g, seq // tk, tk), lambda b, h, i: (b, h, 0, 0))],
        out_specs=pl.BlockSpec((1, tq, width), lambda b, h, i: (b, i, h)),
        out_shape=jax.ShapeDtypeStruct((bsz, seq, d), out_dtype),
        scratch_shapes=[pltpu.VMEM((g, tq, LANE), F32), pltpu.VMEM((g, tq, LANE), F32),
                        pltpu.VMEM((g, tq, head_dim + LANE), F32)],
        compiler_params=_params(("arbitrary", "arbitrary", "arbitrary"), blk),
        name="fox_attention",
    )(q, kv, kv, fk)


def _forget_kernel(h_ref, w_ref, b_ref, o_ref, carry_ref, *, ts):
    @pl.when(pl.program_id(1) == 0)
    def _():
        carry_ref[...] = jnp.zeros_like(carry_ref)

    logit = jnp.dot(h_ref[0], w_ref[...].astype(BF16), preferred_element_type=F32) + b_ref[...]
    log_f = -_softplus(-logit) * LOG2E
    row = lax.broadcasted_iota(jnp.int32, (ts, ts), 0)
    col = lax.broadcasted_iota(jnp.int32, (ts, ts), 1)
    upto = (col <= row).astype(F32)
    csum = jnp.dot(upto, log_f, preferred_element_type=F32,
                   precision=lax.Precision.HIGHEST) + carry_ref[...]
    o_ref[0] = csum
    carry_ref[...] = csum[ts - 1:ts, :]


def _forget_cumsum(h3, w_f, b_f):
    bsz, seq, d = h3.shape
    n_heads = w_f.shape[1]
    lanes = FORGET_LANES
    w_pad = jnp.zeros((d, lanes), F32).at[:, :n_heads].set(w_f)
    b_pad = jnp.zeros((1, lanes), F32).at[0, :n_heads].set(b_f.astype(F32))
    ts = _pick(seq, (256, 128, 64, 32, 16, 8))
    blk = 2 * (_nbytes((ts, d), BF16) + _nbytes((d, lanes), F32) + _nbytes((ts, lanes), F32)) \
        + 4 * _nbytes((ts, ts), F32)
    out = pl.pallas_call(
        functools.partial(_forget_kernel, ts=ts),
        grid=(bsz, seq // ts),
        in_specs=[pl.BlockSpec((1, ts, d), lambda b, i: (b, i, 0)),
                  pl.BlockSpec((d, lanes), lambda b, i: (0, 0)),
                  pl.BlockSpec((1, lanes), lambda b, i: (0, 0))],
        out_specs=pl.BlockSpec((1, ts, lanes), lambda b, i: (b, i, 0)),
        out_shape=jax.ShapeDtypeStruct((bsz, seq, lanes), F32),
        scratch_shapes=[pltpu.VMEM((1, lanes), F32)],
        compiler_params=_params(("arbitrary", "arbitrary"), blk),
        name="forget_cumsum",
    )(h3, w_pad, b_pad)
    return out[:, :, :n_heads]


def _ln_rows(v, g, b):
    mu = jnp.mean(v, axis=-1, keepdims=True)
    cen = v - mu
    var = jnp.mean(cen * cen, axis=-1, keepdims=True)
    return cen * lax.rsqrt(var + LN_EPS) * g + b


PACKED = jnp.uint32


def _pack_halves(h):
    half = h.shape[1] // 2
    lo = lax.bitcast_convert_type(h[:, :half].astype(BF16).astype(F32), jnp.uint32)
    hi = lax.bitcast_convert_type(h[:, half:].astype(BF16).astype(F32), jnp.uint32)
    return hi | (lo >> 16)


def _unpack_halves(u):
    lo = lax.bitcast_convert_type(u << 16, F32).astype(BF16)
    hi = lax.bitcast_convert_type(u & jnp.uint32(0xFFFF0000), F32).astype(BF16)
    return jnp.concatenate([lo, hi], axis=1)


def _emit_mods(xn, mod_refs, out_refs):
    for m in range(len(out_refs)):
        sh_ref, sc_ref = mod_refs[2 * m], mod_refs[2 * m + 1]
        h = xn * sc_ref[0] + sh_ref[0]
        if out_refs[m].dtype == PACKED:
            out_refs[m][...] = _pack_halves(h)
        else:
            out_refs[m][...] = h.astype(out_refs[m].dtype)


def _split_bf16(v):
    hi = v.astype(BF16)
    return hi, (v - hi.astype(F32)).astype(BF16)


def _route_top2(h, w_ref, b_ref, idx_ref, wgt_ref, n_experts):
    h_hi, h_lo = _split_bf16(h)
    w_hi, w_lo = _split_bf16(w_ref[...])
    logits = (jnp.dot(h_hi, w_hi, preferred_element_type=F32)
              + jnp.dot(h_lo, w_hi, preferred_element_type=F32)
              + jnp.dot(h_hi, w_lo, preferred_element_type=F32)) + b_ref[...]
    lane = lax.broadcasted_iota(jnp.int32, logits.shape, 1)
    logits = jnp.where(lane < n_experts, logits, -jnp.inf)
    big = jnp.int32(ROUTER_LANES)
    v1 = jnp.max(logits, axis=1, keepdims=True)
    i1 = jnp.min(jnp.where(logits == v1, lane, big), axis=1, keepdims=True)
    rest = jnp.where(lane == i1, -jnp.inf, logits)
    v2 = jnp.max(rest, axis=1, keepdims=True)
    i2 = jnp.min(jnp.where(rest == v2, lane, big), axis=1, keepdims=True)
    e2 = jnp.exp(v2 - v1)
    w1 = 1.0 / (1.0 + e2)
    w2 = e2 / (1.0 + e2)
    idx_ref[...] = jnp.where(lane == 0, i1, jnp.where(lane == 1, i2, 0))
    wgt_ref[...] = jnp.where(lane == 0, w1, jnp.where(lane == 1, w2, 0.0))


def _ln_kernel(*refs, alpha, n_mod, n_experts):
    x_ref, y_ref, gate_ref, g_ref, b_ref = refs[:5]
    mod_refs = refs[5:5 + 2 * n_mod]
    n_in = 5 + 2 * n_mod + (2 if n_experts else 0)
    xo_ref = refs[n_in]
    h_refs = refs[n_in + 1:n_in + 1 + n_mod]
    v = alpha * x_ref[...] + gate_ref[0] * y_ref[...].astype(F32)
    xn = _ln_rows(v, g_ref[...], b_ref[...])
    xo_ref[...] = xn
    _emit_mods(xn, mod_refs, h_refs)
    if n_experts:
        wr_ref, br_ref = refs[n_in - 2:n_in]
        idx_ref, wgt_ref = refs[n_in + 1 + n_mod:]
        _route_top2(xn * mod_refs[1][0] + mod_refs[0][0], wr_ref, br_ref, idx_ref, wgt_ref,
                    n_experts)


def _residual_ln(x2, y2, gate1, ln_g, ln_b, mods, mod_dtypes, seq, alpha, router=None):
    t, d = x2.shape
    n_mod = len(mods)
    tm = _pick(seq, (256, 128, 64, 32, 16, 8))
    tpb = seq // tm
    tile = pl.BlockSpec((tm, d), lambda i: (i, 0))
    vec = pl.BlockSpec((1, 1, d), lambda i: (i // tpb, 0, 0))
    par = pl.BlockSpec((1, d), lambda i: (0, 0))
    in_specs = [tile, tile, vec, par, par] + [vec, vec] * n_mod
    args = [x2, y2, gate1, ln_g.reshape(1, d), ln_b.reshape(1, d)]
    for sh, sc1 in mods:
        args += [sh, sc1]
    def width(dt):
        return d // 2 if dt == PACKED else d

    out_specs = [tile] + [pl.BlockSpec((tm, width(dt)), lambda i: (i, 0)) for dt in mod_dtypes]
    out_shape = [jax.ShapeDtypeStruct((t, d), F32)] + \
        [jax.ShapeDtypeStruct((t, width(dt)), dt) for dt in mod_dtypes]
    n_experts = 0
    if router is not None:
        w_router, b_router = router
        n_experts = w_router.shape[1]
        lanes = ROUTER_LANES
        args += [jnp.zeros((d, lanes), F32).at[:, :n_experts].set(w_router),
                 jnp.zeros((1, lanes), F32).at[0, :n_experts].set(b_router.astype(F32))]
        in_specs += [pl.BlockSpec((d, lanes), lambda i: (0, 0)),
                     pl.BlockSpec((1, lanes), lambda i: (0, 0))]
        out_specs += [pl.BlockSpec((tm, lanes), lambda i: (i, 0))] * 2
        out_shape += [jax.ShapeDtypeStruct((t, lanes), jnp.int32),
                      jax.ShapeDtypeStruct((t, lanes), F32)]
    blk = 2 * (_nbytes((tm, d), F32) + _nbytes((tm, d), y2.dtype) + _nbytes((tm, d), F32)
               + sum(_nbytes((tm, d), dt) for dt in mod_dtypes)) + 6 * _nbytes((tm, d), F32)
    outs = pl.pallas_call(
        functools.partial(_ln_kernel, alpha=alpha, n_mod=n_mod, n_experts=n_experts),
        grid=(t // tm,),
        in_specs=in_specs,
        out_specs=out_specs,
        out_shape=out_shape,
        compiler_params=_params(("arbitrary",), blk),
        name="residual_ln",
    )(*args)
    hs = list(outs[1:1 + n_mod])
    if router is not None:
        return outs[0], hs, outs[-2][:, :TOP_K], outs[-1][:, :TOP_K]
    return outs[0], hs


def _route_plan(top_idx, n_experts, tile):
    t = top_idx.shape[0]
    p = TOP_K * t
    e_flat = top_idx.reshape(p)
    onehot = (e_flat[:, None] == jnp.arange(n_experts, dtype=jnp.int32)[None, :]).astype(jnp.int32)
    csum = jnp.cumsum(onehot, axis=0)
    counts = csum[-1]
    rank = jnp.sum(csum * onehot, axis=1) - 1
    padded = ((counts + tile - 1) // tile) * tile
    gend = jnp.cumsum(padded)
    gstart = gend - padded
    pos = jnp.sum(onehot * gstart[None, :], axis=1) + rank
    n_rows = ((p + tile - 1) // tile + n_experts) * tile
    row_token = jnp.zeros((n_rows,), jnp.int32).at[pos].set(
        jnp.arange(p, dtype=jnp.int32) // TOP_K)
    tile_start = jnp.arange(n_rows // tile, dtype=jnp.int32) * tile
    tile_expert = jnp.sum((tile_start[:, None] >= gend[None, :]).astype(jnp.int32), axis=1)
    tile_valid = (tile_start < gend[-1]).astype(jnp.int32)
    tile_expert = jnp.minimum(tile_expert, n_experts - 1)
    return pos.astype(jnp.int32), row_token, gend[-1:].astype(jnp.int32), tile_expert, tile_valid


def _row_copy(src_ref, src_row, dst_ref, dst_row, sem):
    return pltpu.make_async_copy(src_ref.at[pl.ds(src_row, 1)], dst_ref.at[pl.ds(dst_row, 1)], sem)


GATHER_UNROLL = 8
DMA_PRIORITIES = 2


def _prefetched_gather(issue, n_steps):
    i = pl.program_id(0)
    slot = i % 2

    @pl.when(i == 0)
    def _():
        issue(i, slot)

    @pl.when(i + 1 < n_steps)
    def _():
        issue(i + 1, 1 - slot)

    return slot


def _dispatch_kernel(tok_ref, total_ref, src_ref, o_ref, buf_ref, sems, *, rows, n_steps):
    def issue(step, slot):
        base = step * rows

        @pl.when(base < total_ref[0])
        def _():
            def body(r2, c):
                for k in range(DMA_PRIORITIES):
                    r = r2 * DMA_PRIORITIES + k
                    _row_copy(src_ref, tok_ref[base + r], buf_ref.at[slot], r,
                              sems.at[slot]).start(priority=k)
                return c

            lax.fori_loop(0, rows // DMA_PRIORITIES, body, 0,
                          unroll=min(GATHER_UNROLL // DMA_PRIORITIES, rows // DMA_PRIORITIES))

    slot = _prefetched_gather(issue, n_steps)
    used = pl.program_id(0) * rows < total_ref[0]

    @pl.when(used)
    def _():
        def drain(r, c):
            _row_copy(src_ref, 0, buf_ref.at[slot], r, sems.at[slot]).wait()
            return c

        lax.fori_loop(0, rows, drain, 0, unroll=min(GATHER_UNROLL, rows))
        o_ref[...] = _unpack_halves(buf_ref[slot])

    @pl.when(jnp.logical_not(used))
    def _():
        o_ref[...] = jnp.zeros_like(o_ref)


def _dispatch(h_packed, row_token, total_rows, rows):
    t, half = h_packed.shape
    d = 2 * half
    n_rows = row_token.shape[0]
    n_steps = n_rows // rows
    blk = 2 * _nbytes((rows, d), BF16) + 4 * _nbytes((rows, half), PACKED)
    grid_spec = pltpu.PrefetchScalarGridSpec(
        num_scalar_prefetch=2,
        grid=(n_steps,),
        in_specs=[pl.BlockSpec(memory_space=pl.ANY)],
        out_specs=pl.BlockSpec((rows, d), lambda i, tok, total: (i, 0)),
        scratch_shapes=[pltpu.VMEM((2, rows, half), PACKED), pltpu.SemaphoreType.DMA((2,))],
    )
    return pl.pallas_call(
        functools.partial(_dispatch_kernel, rows=rows, n_steps=n_steps),
        grid_spec=grid_spec,
        out_shape=jax.ShapeDtypeStruct((n_rows, d), BF16),
        compiler_params=_params(("arbitrary",), blk),
        name="expert_dispatch",
    )(row_token, total_rows, h_packed)


def _combine_ln_kernel(*refs, alpha, n_mod, tm, n_steps):
    pos_ref, x_ref, rows_ref, wgt_ref, gate_ref, g_ref, b_ref = refs[:7]
    mod_refs = refs[7:7 + 2 * n_mod]
    xo_ref = refs[7 + 2 * n_mod]
    h_refs = refs[8 + 2 * n_mod:8 + 3 * n_mod]
    buf_ref, sems = refs[8 + 3 * n_mod:]

    def issue(step, slot):
        base = step * tm

        def body(r, c):
            for k in range(TOP_K):
                _row_copy(rows_ref, pos_ref[(base + r) * TOP_K + k], buf_ref.at[slot, k], r,
                          sems.at[slot]).start(priority=k % DMA_PRIORITIES)
            return c

        lax.fori_loop(0, tm, body, 0, unroll=min(GATHER_UNROLL, tm))

    slot = _prefetched_gather(issue, n_steps)

    def drain(r, c):
        for k in range(TOP_K):
            _row_copy(rows_ref, 0, buf_ref.at[slot, k], r, sems.at[slot]).wait()
        return c

    lax.fori_loop(0, tm, drain, 0, unroll=min(GATHER_UNROLL, tm))
    wgt = wgt_ref[...]
    y = wgt[:, 0:1] * buf_ref[slot, 0]
    for k in range(1, TOP_K):
        y = y + wgt[:, k:k + 1] * buf_ref[slot, k]
    v = alpha * x_ref[...] + gate_ref[0] * y
    xn = _ln_rows(v, g_ref[...], b_ref[...])
    xo_ref[...] = xn
    _emit_mods(xn, mod_refs, h_refs)


def _combine_ln(x2, expert_rows, pos, top_w, gate1, ln_g, ln_b, mods, mod_dtypes, seq, alpha):
    t, d = x2.shape
    n_mod = len(mods)
    tm = _pick(seq, (128, 64, 32, 16, 8))
    tpb = seq // tm
    n_steps = t // tm
    tile = pl.BlockSpec((tm, d), lambda i, pos: (i, 0))
    vec = pl.BlockSpec((1, 1, d), lambda i, pos: (i // tpb, 0, 0))
    par = pl.BlockSpec((1, d), lambda i, pos: (0, 0))
    wgt = pl.BlockSpec((tm, TOP_K), lambda i, pos: (i, 0))
    in_specs = [tile, pl.BlockSpec(memory_space=pl.ANY), wgt, vec, par, par] + [vec, vec] * n_mod
    args = [x2, expert_rows, top_w, gate1, ln_g.reshape(1, d), ln_b.reshape(1, d)]
    for sh, sc1 in mods:
        args += [sh, sc1]
    out_shape = [jax.ShapeDtypeStruct((t, d), F32)] + \
        [jax.ShapeDtypeStruct((t, d), dt) for dt in mod_dtypes]
    blk = 2 * (2 * _nbytes((tm, d), F32) + sum(_nbytes((tm, d), dt) for dt in mod_dtypes)) \
        + (2 * TOP_K + 4) * _nbytes((tm, d), F32)
    grid_spec = pltpu.PrefetchScalarGridSpec(
        num_scalar_prefetch=1,
        grid=(n_steps,),
        in_specs=in_specs,
        out_specs=[tile] * (1 + n_mod),
        scratch_shapes=[pltpu.VMEM((2, TOP_K, tm, d), F32), pltpu.SemaphoreType.DMA((2,))],
    )
    outs = pl.pallas_call(
        functools.partial(_combine_ln_kernel, alpha=alpha, n_mod=n_mod, tm=tm, n_steps=n_steps),
        grid_spec=grid_spec,
        out_shape=out_shape,
        compiler_params=_params(("arbitrary",), blk),
        name="combine_ln",
    )(pos, *args)
    return outs[0], list(outs[1:])


def kernel(x, c, ada_w, ada_b, ada_table, kv_table, a_w_qkv, a_w_o, kv_w, kv_b_f, b_w_q, b_w_o,
           ln_g, ln_b, ffn_w_up, ffn_w_down, moe_w_router, moe_b_router, moe_w_up, moe_w_down):
    bsz, seq, d = x.shape
    t = bsz * seq
    depth, n_mod = ada_table.shape[0], ada_table.shape[1]
    n_a = a_w_qkv.shape[0]
    n_heads = kv_w.shape[1] - 2 * d
    head_dim = d // n_heads
    n_experts = moe_w_router.shape[-1]
    alpha = (2.0 * depth) ** 0.25
    expert_tile = _pick(TOP_K * t, (512, 256, 128, 64, 32, 16))
    q_scale = head_dim ** -0.5 * LOG2E
    stick_heads = STICK_HEADS_PER_STEP if n_heads % STICK_HEADS_PER_STEP == 0 else 1
    fox_heads = FOX_HEADS_PER_STEP if n_heads % FOX_HEADS_PER_STEP == 0 else 1

    mod = _ada_mod(c, ada_w, ada_b).reshape(bsz, n_mod, d)

    def vec3(v):
        return v.reshape(bsz, 1, d)

    def layer_mods(l):
        m = mod + ada_table[l][None]
        return [m[:, i] for i in range(n_mod)]

    def mix_mod(l):
        m = layer_mods(l)
        return vec3(m[0]), vec3(1.0 + m[1])

    kv_mod = (vec3(mod[:, 0] + kv_table[0]), vec3(1.0 + mod[:, 1] + kv_table[1]))

    x2 = x.reshape(t, d)
    h = _modcast(x2, *mix_mod(0), seq, BF16)
    h_kv = None
    shared = None
    for l in range(depth):
        m = layer_mods(l)
        gate_mix1, gate_ffn1 = vec3(1.0 + m[2]), vec3(1.0 + m[5])
        ffn_mod = (vec3(m[3]), vec3(1.0 + m[4]))
        is_moe = l % 2 == 1

        if l < n_a:
            qkv = _matmul(h, a_w_qkv, l, 0, 3 * d, BF16, scaled_cols=d, col_scale=-q_scale)
            o = _stick_attention(qkv.reshape(bsz, seq, 3 * d), n_heads, head_dim, BF16,
                                 stick_heads)
            mix = _matmul(o.reshape(t, d), a_w_o, l, 0, d, BRANCH_DTYPE)
        else:
            if shared is None:
                kv = _matmul(h_kv, kv_w.T, None, 0, 2 * d, BF16, transposed=True)
                log_f = _forget_cumsum(h_kv.reshape(bsz, seq, d), kv_w[:, 2 * d:], kv_b_f)
                shared = (kv.reshape(bsz, seq, 2 * d), jnp.transpose(log_f, (0, 2, 1)))
            j = l - n_a
            q = _matmul(h, b_w_q, j, 0, d, BF16, scaled_cols=d, col_scale=q_scale)
            o = _fox_attention(q.reshape(bsz, seq, d), shared[0], shared[1],
                               n_heads, head_dim, BF16, fox_heads)
            mix = _matmul(o.reshape(t, d), b_w_o, j, 0, d, BRANCH_DTYPE)
        if is_moe:
            x2, (h_ffn,), top_idx, top_w = _residual_ln(
                x2, mix, gate_mix1, ln_g[l, 0], ln_b[l, 0], [ffn_mod], [PACKED], seq, alpha,
                router=(moe_w_router[l // 2], moe_b_router[l // 2]))
        else:
            x2, (h_ffn,) = _residual_ln(x2, mix, gate_mix1, ln_g[l, 0], ln_b[l, 0], [ffn_mod],
                                        [BF16], seq, alpha)

        next_mods, next_dtypes = [], []
        if l + 1 < depth:
            next_mods.append(mix_mod(l + 1))
            next_dtypes.append(BF16)
            if l + 1 == n_a:
                next_mods.append(kv_mod)
                next_dtypes.append(BF16)
        if not is_moe:
            act = _swiglu_up(h_ffn, ffn_w_up, l // 2)
            y = _matmul(act, ffn_w_down, l // 2, 0, d, BRANCH_DTYPE)
            x2, hs = _residual_ln(x2, y, gate_ffn1, ln_g[l, 1], ln_b[l, 1], next_mods,
                                  next_dtypes, seq, alpha)
        else:
            e = l // 2
            pos, row_token, total_rows, tile_expert, tile_valid = _route_plan(
                top_idx, n_experts, expert_tile)
            hg = _dispatch(h_ffn, row_token, total_rows, min(expert_tile, 256))
            next_group = _next_group_tile(tile_expert)
            act = _grouped_swiglu_up(hg, moe_w_up, e, tile_expert, tile_valid, next_group,
                                     expert_tile)
            rows = _grouped_matmul(act, moe_w_down, e, tile_expert, tile_valid, next_group,
                                   expert_tile, F32)
            x2, hs = _combine_ln(x2, rows, pos, top_w, gate_ffn1, ln_g[l, 1], ln_b[l, 1],
                                 next_mods, next_dtypes, seq, alpha)
        if hs:
            h = hs[0]
            if len(hs) > 1:
                h_kv = hs[1]
    return x2.reshape(bsz, seq, d)
```
